```python
import jax, jax.numpy as jnp
from jax import lax
import numpy as np

D_MODEL = 1024
BATCH = 8
SEQ = 8192
DEPTH = 1
DEC_BATCH = 16
DEC_SEQ = 32
PAST_LEN = 2048

CHUNK = 64
D_MIX = D_MODEL
D_POOL = D_MIX // 2
D_HG = D_MIX - D_POOL
POOL_WINDOWS = (2, 4, 8, 16)
N_POOL_GROUPS = len(POOL_WINDOWS)
POOL_GROUP = D_POOL // N_POOL_GROUPS
POOL_HIST = max(POOL_WINDOWS) - 1
HG_HEAD_DIM = 128
HG_HEADS = D_HG // HG_HEAD_DIM
D_IN = D_POOL + 4 * D_HG
D_FF = -(-8 * D_MODEL // (3 * 256)) * 256
EPS = 1e-6

kernel_name = "hymba_pool_hgrn2_stream_step"


def rmsnorm(x, g):
    xf = x.astype(jnp.float32)
    y = xf * lax.rsqrt(jnp.mean(xf * xf, axis=-1, keepdims=True) + EPS)
    return (y * g.astype(jnp.float32)).astype(x.dtype)


def pool_mix(u_ext, n_valid, w_pool, pool_scale):
    B = u_ext.shape[0]
    L = u_ext.shape[1] - POOL_HIST
    uf = u_ext.astype(jnp.float32)
    csum = jnp.concatenate([jnp.zeros_like(uf[:, :1]), jnp.cumsum(uf, axis=1)], axis=1)
    u_cur = uf[:, POOL_HIST:]
    t = jnp.arange(L, dtype=jnp.float32)
    outs = []
    for g, w in enumerate(POOL_WINDOWS):
        sl = slice(g * POOL_GROUP, (g + 1) * POOL_GROUP)
        hi = csum[:, POOL_HIST + 1:, sl]
        lo = csum[:, POOL_HIST + 1 - w:POOL_HIST + 1 - w + L, sl]
        count = jnp.minimum(t + 1.0 + n_valid, float(w))
        outs.append((hi - lo) / count[None, :, None] - u_cur[:, :, sl])
    d = jnp.stack(outs, axis=2)
    y = jnp.einsum('blgc,gcd->blgd', d, w_pool.astype(jnp.float32)).reshape(B, L, D_POOL)
    return (y * pool_scale.astype(jnp.float32)).astype(u_ext.dtype)


def hgrn_chunk(S0, q, k, v, logf):
    C = q.shape[2]
    b = jnp.cumsum(logf, axis=2)
    causal = jnp.tril(jnp.ones((C, C), dtype=bool))
    diff = b[:, :, :, None, :] - b[:, :, None, :, :]
    decay = jnp.exp(jnp.where(causal[None, None, :, :, None], diff, -jnp.inf))
    attn = jnp.einsum('bhtk,bhsk,bhtsk->bhts', q, k, decay)
    o = jnp.einsum('bhts,bhsv->bhtv', attn, v) + jnp.einsum('bhtk,bhkv->bhtv', q * jnp.exp(b), S0)
    b_last = b[:, :, -1:, :]
    k_dec = k * jnp.exp(b_last - b)
    S1 = jnp.exp(b_last[:, :, 0, :])[..., None] * S0 + jnp.einsum('bhsk,bhsv->bhkv', k_dec, v)
    return S1, o


def hgrn_run(S0, q, k, v, logf):
    B, H, L, K = q.shape
    if L <= CHUNK:
        return hgrn_chunk(S0, q, k, v, logf)
    n = L // CHUNK

    def to_chunks(a):
        return a.reshape(B, H, n, CHUNK, a.shape[-1]).transpose(2, 0, 1, 3, 4)

    def step(S, xs):
        return hgrn_chunk(S, *xs)

    S_fin, o = lax.scan(step, S0, (to_chunks(q), to_chunks(k), to_chunks(v), to_chunks(logf)))
    o = o.transpose(1, 2, 0, 3, 4).reshape(B, H, L, -1)
    return S_fin, o


def to_heads(a):
    B, L, _ = a.shape
    return a.reshape(B, L, HG_HEADS, HG_HEAD_DIM).transpose(0, 2, 1, 3)


def layer(x, pool_hist, n_valid, S0, g_pre_mix, w_in, w_pool, pool_scale, lb, g_hg_norm,
          w_out, g_post_mix, g_pre_ffn, w_gate, w_up, w_down, g_post_ffn):
    B, L, _ = x.shape
    h = rmsnorm(x, g_pre_mix)
    z = h @ w_in
    u, zq, zf, zi, zg = jnp.split(z, [D_POOL, D_POOL + D_HG, D_POOL + 2 * D_HG, D_POOL + 3 * D_HG], axis=-1)
    u_ext = jnp.concatenate([pool_hist.astype(u.dtype), u], axis=1)
    y_pool = pool_mix(u_ext, n_valid, w_pool, pool_scale)
    new_pool = u_ext[:, -POOL_HIST:]
    lbf = lb.astype(jnp.float32)
    zff = zf.astype(jnp.float32)
    q = jax.nn.silu(zq.astype(jnp.float32))
    fgate = lbf + (1.0 - lbf) * jax.nn.sigmoid(zff)
    kin = (1.0 - lbf) * jax.nn.sigmoid(-zff)
    logf = jnp.log(fgate)
    S1, o = hgrn_run(S0.astype(jnp.float32), to_heads(q), to_heads(kin),
                     to_heads(zi.astype(jnp.float32)), to_heads(logf))
    o = rmsnorm(o.transpose(0, 2, 1, 3), g_hg_norm).reshape(B, L, D_HG)
    y_hg = (o * jax.nn.silu(zg.astype(jnp.float32))).astype(x.dtype)
    y_mix = jnp.concatenate([y_pool.astype(x.dtype), y_hg], axis=-1) @ w_out
    x = x + rmsnorm(y_mix, g_post_mix)
    h2 = rmsnorm(x, g_pre_ffn)
    f = (jax.nn.silu(h2 @ w_gate) * (h2 @ w_up)) @ w_down
    x = x + rmsnorm(f, g_post_ffn)
    return x, new_pool, S1


def setup_inputs(seed: int = 0) -> dict:
    key = jax.random.key(seed)
    ks = jax.random.split(key, 20)
    f32 = jnp.float32

    def nrm(k, shape, scale):
        return jax.random.normal(k, shape, f32) * scale

    return {
        "x_prompt": nrm(ks[0], (BATCH, SEQ, D_MODEL), 1.0),
        "x_sample": nrm(ks[1], (DEC_BATCH, DEC_SEQ, D_MODEL), 1.0),
        "cache_pool": nrm(ks[2], (DEPTH, DEC_BATCH, POOL_HIST, D_POOL), 1.0),
        "state_hgrn": nrm(ks[3], (DEPTH, DEC_BATCH, HG_HEADS, HG_HEAD_DIM, HG_HEAD_DIM), 0.5),
        "g_pre_mix": 1.0 + nrm(ks[4], (DEPTH, D_MODEL), 0.05),
        "w_in": nrm(ks[5], (DEPTH, D_MODEL, D_IN), D_MODEL ** -0.5),
        "w_pool": nrm(ks[6], (DEPTH, N_POOL_GROUPS, POOL_GROUP, POOL_GROUP), POOL_GROUP ** -0.5),
        "pool_scale": 1.0 + nrm(ks[7], (DEPTH, D_POOL), 0.05),
        "lb_logits": nrm(ks[8], (DEPTH + 1, D_HG), 0.5),
        "g_hg_norm": 1.0 + nrm(ks[9], (DEPTH, HG_HEAD_DIM), 0.05),
        "w_out": nrm(ks[10], (DEPTH, D_MIX, D_MODEL), D_MIX ** -0.5),
        "g_post_mix": 1.0 + nrm(ks[11], (DEPTH, D_MODEL), 0.05),
        "g_pre_ffn": 1.0 + nrm(ks[12], (DEPTH, D_MODEL), 0.05),
        "w_gate": nrm(ks[13], (DEPTH, D_MODEL, D_FF), D_MODEL ** -0.5),
        "w_up": nrm(ks[14], (DEPTH, D_MODEL, D_FF), D_MODEL ** -0.5),
        "w_down": nrm(ks[15], (DEPTH, D_FF, D_MODEL), D_FF ** -0.5),
        "g_post_ffn": 1.0 + nrm(ks[16], (DEPTH, D_MODEL), 0.05),
    }


def reference(x_prompt, x_sample, cache_pool, state_hgrn, g_pre_mix, w_in, w_pool, pool_scale,
              lb_logits, g_hg_norm, w_out, g_post_mix, g_pre_ffn, w_gate, w_up, w_down, g_post_ffn):
    lb_all = jnp.cumsum(jax.nn.softmax(lb_logits.astype(jnp.float32), axis=0), axis=0)
    xp = x_prompt
    xs = x_sample
    pools_p, hgrn_p, pools_s, hgrn_s = [], [], [], []
    for l in range(DEPTH):
        ws = (g_pre_mix[l], w_in[l], w_pool[l], pool_scale[l], lb_all[l], g_hg_norm[l], w_out[l],
              g_post_mix[l], g_pre_ffn[l], w_gate[l], w_up[l], w_down[l], g_post_ffn[l])
        hist0 = jnp.zeros((xp.shape[0], POOL_HIST, D_POOL), xp.dtype)
        S0 = jnp.zeros((xp.shape[0], HG_HEADS, HG_HEAD_DIM, HG_HEAD_DIM), jnp.float32)
        xp, np_p, S_p = layer(xp, hist0, 0, S0, *ws)
        xs, np_s, S_s = layer(xs, cache_pool[l], POOL_HIST, state_hgrn[l], *ws)
        pools_p.append(np_p)
        hgrn_p.append(S_p)
        pools_s.append(np_s)
        hgrn_s.append(S_s)
    new_pool_prompt = jnp.stack(pools_p, axis=0)
    new_hgrn_prompt = jnp.stack(hgrn_p, axis=0)
    new_pool_sample = jnp.stack(pools_s, axis=0)
    new_hgrn_sample = jnp.stack(hgrn_s, axis=0)
    return (xp, xs, new_pool_prompt, new_hgrn_prompt, new_pool_sample, new_hgrn_sample)
```

```python
import functools

import jax
import jax.numpy as jnp
from jax import lax
from jax.experimental import pallas as pl
from jax.experimental.pallas import tpu as pltpu

D_MODEL = 1024
D_POOL = 512
D_HG = 512
POOL_WINDOWS = (2, 4, 8, 16)
POOL_GROUP = D_POOL // len(POOL_WINDOWS)
POOL_HIST = max(POOL_WINDOWS) - 1
HIST_ROWS = POOL_HIST + 1
HEAD_DIM = 128
N_HEADS = D_HG // HEAD_DIM
D_IN = D_POOL + 4 * D_HG
EPS = 1e-6
PROMPT_CHUNK = 64
FFN_CHUNK = 1024

V7X_VMEM_BYTES = 64 * 1024 * 1024
VMEM_LIMIT_BYTES = V7X_VMEM_BYTES - 8 * 1024 * 1024

_NT = (((1,), (1,)), ((), ()))
_TN = (((0,), (0,)), ((), ()))


def _rms(x, g):
    return x * lax.rsqrt(jnp.mean(x * x, axis=-1, keepdims=True) + EPS) * g


def _bf(x):
    return x.astype(jnp.bfloat16)


def _dot(a, b, dims=None):
    if dims is None:
        return jnp.dot(a, b, preferred_element_type=jnp.float32)
    return lax.dot_general(a, b, dims, preferred_element_type=jnp.float32)


def _layer_kernel(x_ref, hist_ref, s0_ref, lbl_ref, g_pre_mix_ref, w_in_ref, w_pool_ref,
                  pool_scale_ref, g_hg_ref, w_out_ref, g_post_mix_ref, g_pre_ffn_ref,
                  w_gate_ref, w_up_ref, w_down_ref, g_post_ffn_ref,
                  y_ref, pool_out_ref, s_out_ref,
                  z_ref, ext_ref, st_ref, ymix_ref,
                  *, nb, lt, n_valid, layer_idx):
    pid = pl.program_id(0)
    last = pl.num_programs(0) - 1
    m = nb * lt

    @pl.when(pid == 0)
    def _init():
        ext_ref[:, 1:HIST_ROWS, :] = hist_ref[...]
        for i in range(nb):
            for h in range(N_HEADS):
                st_ref[i * N_HEADS + h] = s0_ref[i, h].T

    lbl = lbl_ref[...]
    lbe = jnp.exp(lbl - jnp.max(lbl, axis=0, keepdims=True))
    lb = (jnp.sum(lbe[:layer_idx + 1], axis=0, keepdims=True)
          / jnp.sum(lbe, axis=0, keepdims=True))

    x = x_ref[...].reshape(m, D_MODEL)
    h = _bf(_rms(x, g_pre_mix_ref[...]))
    z_ref[...] = _dot(h, w_in_ref[...])

    t_loc = lax.broadcasted_iota(jnp.int32, (lt, 1), 0)
    pos = (pid * lt + t_loc + 1 + n_valid).astype(jnp.float32)
    inv_cnt = [1.0 / jnp.minimum(pos, float(w)) for w in POOL_WINDOWS]

    ti = lax.broadcasted_iota(jnp.int32, (lt, lt), 0)
    si = lax.broadcasted_iota(jnp.int32, (lt, lt), 1)
    pair_xor = jnp.where(ti > si, ti ^ si, 0)
    diag = ti == si
    row = lax.broadcasted_iota(jnp.int32, (lt, HEAD_DIM), 0)
    g_hg = g_hg_ref[...]
    pool_scale = pool_scale_ref[...]

    def per_seq(b, carry):
        r0 = pl.multiple_of(b * lt, lt)
        rows = pl.ds(r0, lt)

        ext_ref[b, HIST_ROWS:HIST_ROWS + lt, :] = z_ref[rows, 0:D_POOL]
        for g, w in enumerate(POOL_WINDOWS):
            cols = slice(g * POOL_GROUP, (g + 1) * POOL_GROUP)
            cur = ext_ref[b, HIST_ROWS:HIST_ROWS + lt, cols]
            win = cur
            for j in range(1, w):
                win = win + ext_ref[b, HIST_ROWS - j:HIST_ROWS - j + lt, cols]
            d = win * inv_cnt[g] - cur
            yp = _dot(_bf(d), w_pool_ref[g]) * pool_scale[:, cols]
            ymix_ref[rows, cols] = _bf(yp)

        @pl.when(pid == last)
        def _pool_out():
            pool_out_ref[b] = ext_ref[b, lt + 1:lt + HIST_ROWS, :]

        ext_ref[b, 0:HIST_ROWS, :] = ext_ref[b, lt:lt + HIST_ROWS, :]

        for hd in range(N_HEADS):
            c0 = hd * HEAD_DIM
            zq = z_ref[rows, D_POOL + c0:D_POOL + c0 + HEAD_DIM]
            zf = z_ref[rows, D_POOL + D_HG + c0:D_POOL + D_HG + c0 + HEAD_DIM]
            v = z_ref[rows, D_POOL + 2 * D_HG + c0:D_POOL + 2 * D_HG + c0 + HEAD_DIM]
            zg = z_ref[rows, D_POOL + 3 * D_HG + c0:D_POOL + 3 * D_HG + c0 + HEAD_DIM]
            lbh = lb[:, c0:c0 + HEAD_DIM]

            q = zq * jax.nn.sigmoid(zq)
            e = jnp.exp(-jnp.abs(zf))
            r = 1.0 / (1.0 + e)
            er = e * r
            pos_f = zf >= 0.0
            fgate = lbh + (1.0 - lbh) * jnp.where(pos_f, r, er)
            k = (1.0 - lbh) * jnp.where(pos_f, er, r)
            logf = jnp.log(fgate)

            attn = jnp.where(diag, _dot(_bf(q), _bf(k), _NT), 0.0)
            c = logf
            tot = logf
            n = 1
            while n < lt:
                qn = _bf(q * jnp.exp(c))
                kn = _bf(k * jnp.exp(tot - c))
                p = _dot(qn, kn, _NT)
                attn = jnp.where((pair_xor >= n) & (pair_xor < 2 * n), p, attn)
                lower = (row & n) != 0
                dn = pltpu.roll(tot, n, 0)
                up = pltpu.roll(tot, lt - n, 0)
                c = c + jnp.where(lower, dn, 0.0)
                tot = tot + jnp.where(lower, dn, up)
                n *= 2

            st = st_ref[b * N_HEADS + hd]
            o = _dot(_bf(attn), _bf(v)) + _dot(_bf(q * jnp.exp(c)), _bf(st), _NT)
            k_dec = k * jnp.exp(tot - c)
            st_new = st * jnp.exp(tot[0:1, :]) + _dot(_bf(v), _bf(k_dec), _TN)
            st_ref[b * N_HEADS + hd] = st_new

            @pl.when(pid == last)
            def _state_out():
                s_out_ref[b, hd] = st_new.T

            o = _rms(o, g_hg) * (zg * jax.nn.sigmoid(zg))
            ymix_ref[rows, D_POOL + c0:D_POOL + c0 + HEAD_DIM] = _bf(o)
        return carry

    lax.fori_loop(0, nb, per_seq, 0)

    x1 = x + _rms(_dot(ymix_ref[...], w_out_ref[...]), g_post_mix_ref[...])
    h2 = _bf(_rms(x1, g_pre_ffn_ref[...]))
    f = None
    for c0 in range(0, w_gate_ref.shape[1], FFN_CHUNK):
        cols = slice(c0, min(c0 + FFN_CHUNK, w_gate_ref.shape[1]))
        gate = _dot(h2, w_gate_ref[:, cols])
        upp = _dot(h2, w_up_ref[:, cols])
        act = _bf(gate * jax.nn.sigmoid(gate) * upp)
        part = _dot(act, w_down_ref[cols, :])
        f = part if f is None else f + part
    y_ref[...] = (x1 + _rms(f, g_post_ffn_ref[...])).reshape(nb, lt, D_MODEL)


def _const_spec(shape):
    zeros = (0,) * len(shape)
    return pl.BlockSpec(shape, lambda i: zeros, pipeline_mode=pl.Buffered(1))


def _layer_call(x, hist, s0, lb_logits, weights, *, lt, n_valid, layer_idx):
    nb, seq, _ = x.shape
    assert seq % lt == 0 and lt % 8 == 0 and lt >= HIST_ROWS and (lt & (lt - 1)) == 0
    m = nb * lt
    (g_pre_mix, w_in, w_pool, pool_scale, g_hg, w_out, g_post_mix, g_pre_ffn,
     w_gate, w_up, w_down, g_post_ffn) = weights
    consts = (hist, s0, lb_logits, g_pre_mix, w_in, w_pool, pool_scale, g_hg, w_out,
              g_post_mix, g_pre_ffn, w_gate, w_up, w_down, g_post_ffn)
    x_spec = pl.BlockSpec((nb, lt, D_MODEL), lambda i: (0, i, 0))
    kern = functools.partial(_layer_kernel, nb=nb, lt=lt, n_valid=n_valid, layer_idx=layer_idx)
    return pl.pallas_call(
        kern,
        grid=(seq // lt,),
        in_specs=[x_spec] + [_const_spec(c.shape) for c in consts],
        out_specs=[x_spec, _const_spec((nb, POOL_HIST, D_POOL)),
                   _const_spec((nb, N_HEADS, HEAD_DIM, HEAD_DIM))],
        out_shape=[jax.ShapeDtypeStruct(x.shape, x.dtype),
                   jax.ShapeDtypeStruct((nb, POOL_HIST, D_POOL), x.dtype),
                   jax.ShapeDtypeStruct((nb, N_HEADS, HEAD_DIM, HEAD_DIM), jnp.float32)],
        scratch_shapes=[pltpu.VMEM((m, D_IN), jnp.float32),
                        pltpu.VMEM((nb, HIST_ROWS + lt, D_POOL), jnp.float32),
                        pltpu.VMEM((nb * N_HEADS, HEAD_DIM, HEAD_DIM), jnp.float32),
                        pltpu.VMEM((m, D_MODEL), jnp.bfloat16)],
        compiler_params=pltpu.CompilerParams(dimension_semantics=("arbitrary",),
                                             vmem_limit_bytes=VMEM_LIMIT_BYTES),
        name=f"layer_lt{lt}",
    )(x, *consts)


def kernel(x_prompt, x_sample, cache_pool, state_hgrn, g_pre_mix, w_in, w_pool, pool_scale,
           lb_logits, g_hg_norm, w_out, g_post_mix, g_pre_ffn, w_gate, w_up, w_down, g_post_ffn):
    depth = w_in.shape[0]
    xp, xs = x_prompt, x_sample
    pools_p, hgrn_p, pools_s, hgrn_s = [], [], [], []
    for l in range(depth):
        row = lambda a: a[l][None, :]
        weights = (row(g_pre_mix), _bf(w_in[l]), _bf(w_pool[l]), row(pool_scale), row(g_hg_norm),
                   _bf(w_out[l]), row(g_post_mix), row(g_pre_ffn), _bf(w_gate[l]), _bf(w_up[l]),
                   _bf(w_down[l]), row(g_post_ffn))
        hist0 = jnp.zeros((xp.shape[0], POOL_HIST, D_POOL), xp.dtype)
        s00 = jnp.zeros((xp.shape[0], N_HEADS, HEAD_DIM, HEAD_DIM), jnp.float32)
        xp, pool_p, s_p = _layer_call(xp, hist0, s00, lb_logits, weights,
                                      lt=PROMPT_CHUNK, n_valid=0, layer_idx=l)
        xs, pool_s, s_s = _layer_call(xs, cache_pool[l], state_hgrn[l], lb_logits, weights,
                                      lt=xs.shape[1], n_valid=POOL_HIST, layer_idx=l)
        pools_p.append(pool_p)
        hgrn_p.append(s_p)
        pools_s.append(pool_s)
        hgrn_s.append(s_s)
    return (xp, xs, jnp.stack(pools_p), jnp.stack(hgrn_p), jnp.stack(pools_s), jnp.stack(hgrn_s))
```

```python
import functools

import jax
import jax.numpy as jnp
from jax import lax
from jax.experimental import pallas as pl
from jax.experimental.pallas import tpu as pltpu

D_MODEL = 1024
D_POOL = 512
D_HG = 512
POOL_WINDOWS = (2, 4, 8, 16)
POOL_GROUP = D_POOL // len(POOL_WINDOWS)
POOL_HIST = max(POOL_WINDOWS) - 1
HIST_ROWS = POOL_HIST + 1
HEAD_DIM = 128
N_HEADS = D_HG // HEAD_DIM
D_IN = D_POOL + 4 * D_HG
EPS = 1e-6
PROMPT_CHUNK = 64
FFN_CHUNK = 1024

V7X_VMEM_BYTES = 64 * 1024 * 1024
VMEM_LIMIT_BYTES = V7X_VMEM_BYTES - 8 * 1024 * 1024

_NT = (((1,), (1,)), ((), ()))
_TN = (((0,), (0,)), ((), ()))


def _rms(x, g):
    return x * lax.rsqrt(jnp.mean(x * x, axis=-1, keepdims=True) + EPS) * g


def _bf(x):
    return x.astype(jnp.bfloat16)


def _dot(a, b, dims=None):
    if dims is None:
        return jnp.dot(a, b, preferred_element_type=jnp.float32)
    return lax.dot_general(a, b, dims, preferred_element_type=jnp.float32)


def _layer_kernel(x_ref, hist_ref, s0_ref, lbl_ref, g_pre_mix_ref, w_in_ref, w_pool_ref,
                  pool_scale_ref, g_hg_ref, w_out_ref, g_post_mix_ref, g_pre_ffn_ref,
                  w_gate_ref, w_up_ref, w_down_ref, g_post_ffn_ref,
                  y_ref, pool_out_ref, s_out_ref,
                  z_ref, ext_ref, st_ref, ymix_ref,
                  *, nb, lt, n_valid, layer_idx):
    pid = pl.program_id(0)
    last = pl.num_programs(0) - 1
    m = nb * lt

    @pl.when(pid == 0)
    def _init():
        ext_ref[:, 1:HIST_ROWS, :] = hist_ref[...]
        for i in range(nb):
            for h in range(N_HEADS):
                st_ref[i * N_HEADS + h] = s0_ref[i, h].T

    lbl = lbl_ref[...]
    lbe = jnp.exp(lbl - jnp.max(lbl, axis=0, keepdims=True))
    lb = (jnp.sum(lbe[:layer_idx + 1], axis=0, keepdims=True)
          / jnp.sum(lbe, axis=0, keepdims=True))

    x = x_ref[...].reshape(m, D_MODEL)
    h = _bf(_rms(x, g_pre_mix_ref[...]))
    z_ref[...] = _dot(h, w_in_ref[...])

    t_loc = lax.broadcasted_iota(jnp.int32, (lt, 1), 0)
    pos = (pid * lt + t_loc + 1 + n_valid).astype(jnp.float32)
    inv_cnt = [1.0 / jnp.minimum(pos, float(w)) for w in POOL_WINDOWS]

    ti = lax.broadcasted_iota(jnp.int32, (lt, lt), 0)
    si = lax.broadcasted_iota(jnp.int32, (lt, lt), 1)
    pair_xor = jnp.where(ti > si, ti ^ si, 0)
    diag = ti == si
    row = lax.broadcasted_iota(jnp.int32, (lt, HEAD_DIM), 0)
    g_hg = g_hg_ref[...]
    pool_scale = pool_scale_ref[...]

    def per_seq(b, carry):
        r0 = pl.multiple_of(b * lt, lt)
        rows = pl.ds(r0, lt)

        ext_ref[b, HIST_ROWS:HIST_ROWS + lt, :] = z_ref[rows, 0:D_POOL]
        for g, w in enumerate(POOL_WINDOWS):
            cols = slice(g * POOL_GROUP, (g + 1) * POOL_GROUP)
            cur = ext_ref[b, HIST_ROWS:HIST_ROWS + lt, cols]
            win = cur
            for j in range(1, w):
                win = win + ext_ref[b, HIST_ROWS - j:HIST_ROWS - j + lt, cols]
            d = win * inv_cnt[g] - cur
            yp = _dot(_bf(d), w_pool_ref[g]) * pool_scale[:, cols]
            ymix_ref[rows, cols] = _bf(yp)
        ext_ref[b, 0:HIST_ROWS, :] = ext_ref[b, lt:lt + HIST_ROWS, :]

        for hd in range(N_HEADS):
            c0 = hd * HEAD_DIM
            zq = z_ref[rows, D_POOL + c0:D_POOL + c0 + HEAD_DIM]
            zf = z_ref[rows, D_POOL + D_HG + c0:D_POOL + D_HG + c0 + HEAD_DIM]
            v = z_ref[rows, D_POOL + 2 * D_HG + c0:D_POOL + 2 * D_HG + c0 + HEAD_DIM]
            zg = z_ref[rows, D_POOL + 3 * D_HG + c0:D_POOL + 3 * D_HG + c0 + HEAD_DIM]
            lbh = lb[:, c0:c0 + HEAD_DIM]

            q = zq * jax.nn.sigmoid(zq)
            e = jnp.exp(-jnp.abs(zf))
            r = 1.0 / (1.0 + e)
            er = e * r
            pos_f = zf >= 0.0
            fgate = lbh + (1.0 - lbh) * jnp.where(pos_f, r, er)
            k = (1.0 - lbh) * jnp.where(pos_f, er, r)
            logf = jnp.log(fgate)

            attn = jnp.where(diag, _dot(_bf(q), _bf(k), _NT), 0.0)
            c = logf
            tot = logf
            n = 1
            while n < lt:
                qn = _bf(q * jnp.exp(c))
                kn = _bf(k * jnp.exp(tot - c))
                p = _dot(qn, kn, _NT)
                attn = jnp.where((pair_xor >= n) & (pair_xor < 2 * n), p, attn)
                lower = (row & n) != 0
                dn = pltpu.roll(tot, n, 0)
                up = pltpu.roll(tot, lt - n, 0)
                c = c + jnp.where(lower, dn, 0.0)
                tot = tot + jnp.where(lower, dn, up)
                n *= 2

            st = st_ref[b * N_HEADS + hd]
            o = _dot(_bf(attn), _bf(v)) + _dot(_bf(q * jnp.exp(c)), _bf(st), _NT)
            k_dec = k * jnp.exp(tot - c)
            st_new = st * jnp.exp(tot[0:1, :]) + _dot(_bf(v), _bf(k_dec), _TN)
            st_ref[b * N_HEADS + hd] = st_new

            o = _rms(o, g_hg) * (zg * jax.nn.sigmoid(zg))
            ymix_ref[rows, D_POOL + c0:D_POOL + c0 + HEAD_DIM] = _bf(o)
        return carry

    lax.fori_loop(0, nb, per_seq, 0, unroll=2)

    @pl.when(pid == last)
    def _carry_out():
        pool_out_ref[...] = ext_ref[:, 1:HIST_ROWS, :]
        for i in range(nb):
            for hd in range(N_HEADS):
                s_out_ref[i, hd] = st_ref[i * N_HEADS + hd].T

    x1 = x + _rms(_dot(ymix_ref[...], w_out_ref[...]), g_post_mix_ref[...])
    h2 = _bf(_rms(x1, g_pre_ffn_ref[...]))
    f = None
    for c0 in range(0, w_gate_ref.shape[1], FFN_CHUNK):
        cols = slice(c0, min(c0 + FFN_CHUNK, w_gate_ref.shape[1]))
        gate = _dot(h2, w_gate_ref[:, cols])
        upp = _dot(h2, w_up_ref[:, cols])
        act = _bf(gate * jax.nn.sigmoid(gate) * upp)
        part = _dot(act, w_down_ref[cols, :])
        f = part if f is None else f + part
    y_ref[...] = (x1 + _rms(f, g_post_ffn_ref[...])).reshape(nb, lt, D_MODEL)


def _const_spec(shape):
    zeros = (0,) * len(shape)
    return pl.BlockSpec(shape, lambda i: zeros, pipeline_mode=pl.Buffered(1))


def _layer_call(x, hist, s0, lb_logits, weights, *, lt, n_valid, layer_idx):
    nb, seq, _ = x.shape
    assert seq % lt == 0 and lt % 8 == 0 and lt >= HIST_ROWS and (lt & (lt - 1)) == 0
    m = nb * lt
    (g_pre_mix, w_in, w_pool, pool_scale, g_hg, w_out, g_post_mix, g_pre_ffn,
     w_gate, w_up, w_down, g_post_ffn) = weights
    consts = (hist, s0, lb_logits, g_pre_mix, w_in, w_pool, pool_scale, g_hg, w_out,
              g_post_mix, g_pre_ffn, w_gate, w_up, w_down, g_post_ffn)
    x_spec = pl.BlockSpec((nb, lt, D_MODEL), lambda i: (0, i, 0))
    kern = functools.partial(_layer_kernel, nb=nb, lt=lt, n_valid=n_valid, layer_idx=layer_idx)
    return pl.pallas_call(
        kern,
        grid=(seq // lt,),
        in_specs=[x_spec] + [_const_spec(c.shape) for c in consts],
        out_specs=[x_spec, _const_spec((nb, POOL_HIST, D_POOL)),
                   _const_spec((nb, N_HEADS, HEAD_DIM, HEAD_DIM))],
        out_shape=[jax.ShapeDtypeStruct(x.shape, x.dtype),
                   jax.ShapeDtypeStruct((nb, POOL_HIST, D_POOL), x.dtype),
                   jax.ShapeDtypeStruct((nb, N_HEADS, HEAD_DIM, HEAD_DIM), jnp.float32)],
        scratch_shapes=[pltpu.VMEM((m, D_IN), jnp.float32),
                        pltpu.VMEM((nb, HIST_ROWS + lt, D_POOL), jnp.float32),
                        pltpu.VMEM((nb * N_HEADS, HEAD_DIM, HEAD_DIM), jnp.float32),
                        pltpu.VMEM((m, D_MODEL), jnp.bfloat16)],
        compiler_params=pltpu.CompilerParams(dimension_semantics=("arbitrary",),
                                             vmem_limit_bytes=VMEM_LIMIT_BYTES),
        name=f"layer_lt{lt}",
    )(x, *consts)


def kernel(x_prompt, x_sample, cache_pool, state_hgrn, g_pre_mix, w_in, w_pool, pool_scale,
           lb_logits, g_hg_norm, w_out, g_post_mix, g_pre_ffn, w_gate, w_up, w_down, g_post_ffn):
    depth = w_in.shape[0]
    xp, xs = x_prompt, x_sample
    pools_p, hgrn_p, pools_s, hgrn_s = [], [], [], []
    for l in range(depth):
        row = lambda a: a[l][None, :]
        weights = (row(g_pre_mix), _bf(w_in[l]), _bf(w_pool[l]), row(pool_scale), row(g_hg_norm),
                   _bf(w_out[l]), row(g_post_mix), row(g_pre_ffn), _bf(w_gate[l]), _bf(w_up[l]),
                   _bf(w_down[l]), row(g_post_ffn))
        hist0 = jnp.zeros((xp.shape[0], POOL_HIST, D_POOL), xp.dtype)
        s00 = jnp.zeros((xp.shape[0], N_HEADS, HEAD_DIM, HEAD_DIM), jnp.float32)
        xp, pool_p, s_p = _layer_call(xp, hist0, s00, lb_logits, weights,
                                      lt=PROMPT_CHUNK, n_valid=0, layer_idx=l)
        xs, pool_s, s_s = _layer_call(xs, cache_pool[l], state_hgrn[l], lb_logits, weights,
                                      lt=xs.shape[1], n_valid=POOL_HIST, layer_idx=l)
        pools_p.append(pool_p)
        hgrn_p.append(s_p)
        pools_s.append(pool_s)
        hgrn_s.append(s_s)
    return (xp, xs, jnp.stack(pools_p), jnp.stack(hgrn_p), jnp.stack(pools_s), jnp.stack(hgrn_s))
```

```python
import functools
import math

import jax
import jax.numpy as jnp
from jax import lax
from jax.experimental import pallas as pl
from jax.experimental.pallas import tpu as pltpu

D_MODEL = 1024
D_POOL = 512
D_HG = 512
POOL_WINDOWS = (2, 4, 8, 16)
POOL_GROUP = D_POOL // len(POOL_WINDOWS)
POOL_HIST = max(POOL_WINDOWS) - 1
HIST_ROWS = POOL_HIST + 1
HEAD_DIM = 128
N_HEADS = D_HG // HEAD_DIM
D_IN = D_POOL + 4 * D_HG
EPS = 1e-6
PROMPT_CHUNK = 64
FFN_CHUNK = 1024

BASE_BLOCK = 16
MAX_FACTOR_LOG = 40.0
MIDPOINT_MIN_LB = math.exp(-MAX_FACTOR_LOG / (BASE_BLOCK // 2 - 1))

V7X_VMEM_BYTES = 64 * 1024 * 1024
VMEM_LIMIT_BYTES = V7X_VMEM_BYTES - 8 * 1024 * 1024

_NT = (((1,), (1,)), ((), ()))
_TN = (((0,), (0,)), ((), ()))


def _rms(x, g):
    return x * lax.rsqrt(jnp.mean(x * x, axis=-1, keepdims=True) + EPS) * g


def _bf(x):
    return x.astype(jnp.bfloat16)


def _dot(a, b, dims=None):
    if dims is None:
        return jnp.dot(a, b, preferred_element_type=jnp.float32)
    return lax.dot_general(a, b, dims, preferred_element_type=jnp.float32)


def _layer_kernel(x_ref, hist_ref, s0_ref, lbl_ref, g_pre_mix_ref, w_in_ref, w_pool_ref,
                  pool_scale_ref, g_hg_ref, w_out_ref, g_post_mix_ref, g_pre_ffn_ref,
                  w_gate_ref, w_up_ref, w_down_ref, g_post_ffn_ref,
                  y_ref, pool_out_ref, s_out_ref,
                  z_ref, ext_ref, ymix_ref,
                  *, nb, lt, n_valid, layer_idx):
    pid = pl.program_id(0)
    last = pl.num_programs(0) - 1
    m = nb * lt

    @pl.when(pid == 0)
    def _init():
        ext_ref[:, 1:HIST_ROWS, :] = hist_ref[...]
        for i in range(nb):
            for h in range(N_HEADS):
                s_out_ref[i, h] = s0_ref[i, h].T

    lbl = lbl_ref[...]
    lbe = jnp.exp(lbl - jnp.max(lbl, axis=0, keepdims=True))
    lb = (jnp.sum(lbe[:layer_idx + 1], axis=0, keepdims=True)
          / jnp.sum(lbe, axis=0, keepdims=True))

    x = x_ref[...].reshape(m, D_MODEL)
    h = _bf(_rms(x, g_pre_mix_ref[...]))
    ext_ref[:, HIST_ROWS:HIST_ROWS + lt, :] = _dot(h, w_in_ref[:, 0:D_POOL]).reshape(nb, lt, D_POOL)
    z_ref[...] = _dot(h, w_in_ref[:, D_POOL:D_IN])

    t_loc = lax.broadcasted_iota(jnp.int32, (lt, 1), 0)
    pos = (pid * lt + t_loc + 1 + n_valid).astype(jnp.float32)
    inv_cnt = [1.0 / jnp.minimum(pos, float(w)) for w in POOL_WINDOWS]

    ti = lax.broadcasted_iota(jnp.int32, (lt, lt), 0)
    si = lax.broadcasted_iota(jnp.int32, (lt, lt), 1)
    pair_xor = jnp.where(ti > si, ti ^ si, 0)
    diag = ti == si
    in_base_block = (pair_xor < BASE_BLOCK) & (ti >= si)
    row = lax.broadcasted_iota(jnp.int32, (lt, HEAD_DIM), 0)
    g_hg = g_hg_ref[...]
    pool_scale = pool_scale_ref[...]

    def pair_scores(q, k, logf, midpoint):
        attn = None if midpoint else jnp.where(diag, _dot(_bf(q), _bf(k), _NT), 0.0)
        c = logf
        tot = logf
        n = 1
        while n < lt:
            lower = (row & n) != 0
            if midpoint and 2 * n == BASE_BLOCK:
                a = jnp.where(lower, c, c - tot)
                p = _dot(_bf(q * jnp.exp(a)), _bf(k * jnp.exp(-a)), _NT)
                attn = jnp.where(in_base_block, p, 0.0)
            elif not midpoint or n >= BASE_BLOCK:
                p = _dot(_bf(q * jnp.exp(c)), _bf(k * jnp.exp(tot - c)), _NT)
                attn = jnp.where((pair_xor >= n) & (pair_xor < 2 * n), p, attn)
            dn = pltpu.roll(tot, n, 0)
            up = pltpu.roll(tot, lt - n, 0)
            c = c + jnp.where(lower, dn, 0.0)
            tot = tot + jnp.where(lower, dn, up)
            n *= 2
        return attn, c, tot

    def per_seq(b, midpoint):
        rows = pl.ds(pl.multiple_of(b * lt, lt), lt)

        for g, w in enumerate(POOL_WINDOWS):
            cols = slice(g * POOL_GROUP, (g + 1) * POOL_GROUP)
            cur = ext_ref[b, HIST_ROWS:HIST_ROWS + lt, cols]
            win = cur
            for j in range(1, w):
                win = win + ext_ref[b, HIST_ROWS - j:HIST_ROWS - j + lt, cols]
            d = win * inv_cnt[g] - cur
            yp = _dot(_bf(d), w_pool_ref[g]) * pool_scale[:, cols]
            ymix_ref[rows, cols] = _bf(yp)
        ext_ref[b, 0:HIST_ROWS, :] = ext_ref[b, lt:lt + HIST_ROWS, :]

        for hd in range(N_HEADS):
            c0 = hd * HEAD_DIM
            zq = z_ref[rows, c0:c0 + HEAD_DIM]
            zf = z_ref[rows, D_HG + c0:D_HG + c0 + HEAD_DIM]
            v = z_ref[rows, 2 * D_HG + c0:2 * D_HG + c0 + HEAD_DIM]
            zg = z_ref[rows, 3 * D_HG + c0:3 * D_HG + c0 + HEAD_DIM]
            lbh = lb[:, c0:c0 + HEAD_DIM]

            q = zq * jax.nn.sigmoid(zq)
            e = jnp.exp(-jnp.abs(zf))
            r = 1.0 / (1.0 + e)
            er = e * r
            pos_f = zf >= 0.0
            fgate = lbh + (1.0 - lbh) * jnp.where(pos_f, r, er)
            k = (1.0 - lbh) * jnp.where(pos_f, er, r)
            logf = jnp.log(fgate)

            attn, c, tot = pair_scores(q, k, logf, midpoint)

            st = s_out_ref[b, hd]
            o = _dot(_bf(attn), _bf(v)) + _dot(_bf(q * jnp.exp(c)), _bf(st), _NT)
            k_dec = k * jnp.exp(tot - c)
            s_out_ref[b, hd] = st * jnp.exp(tot[0:1, :]) + _dot(_bf(v), _bf(k_dec), _TN)

            o = _rms(o, g_hg) * (zg * jax.nn.sigmoid(zg))
            ymix_ref[rows, D_POOL + c0:D_POOL + c0 + HEAD_DIM] = _bf(o)

    def mixers(midpoint):
        lax.fori_loop(0, nb, lambda b, carry: per_seq(b, midpoint), None, unroll=2)

    lax.cond(jnp.min(lb) >= MIDPOINT_MIN_LB,
             functools.partial(mixers, True), functools.partial(mixers, False))

    @pl.when(pid == last)
    def _carry_out():
        pool_out_ref[...] = ext_ref[:, 1:HIST_ROWS, :]
        for i in range(nb):
            for hd in range(N_HEADS):
                s_out_ref[i, hd] = s_out_ref[i, hd].T

    x1 = x + _rms(_dot(ymix_ref[...], w_out_ref[...]), g_post_mix_ref[...])
    h2 = _bf(_rms(x1, g_pre_ffn_ref[...]))
    f = None
    for c0 in range(0, w_gate_ref.shape[1], FFN_CHUNK):
        cols = slice(c0, min(c0 + FFN_CHUNK, w_gate_ref.shape[1]))
        gate = _dot(h2, w_gate_ref[:, cols])
        upp = _dot(h2, w_up_ref[:, cols])
        act = _bf(gate * jax.nn.sigmoid(gate) * upp)
        part = _dot(act, w_down_ref[cols, :])
        f = part if f is None else f + part
    y_ref[...] = (x1 + _rms(f, g_post_ffn_ref[...])).reshape(nb, lt, D_MODEL)


def _const_spec(shape):
    zeros = (0,) * len(shape)
    return pl.BlockSpec(shape, lambda i: zeros, pipeline_mode=pl.Buffered(1))


def _layer_call(x, hist, s0, lb_logits, weights, *, lt, n_valid, layer_idx):
    nb, seq, _ = x.shape
    assert seq % lt == 0 and lt % BASE_BLOCK == 0 and lt >= HIST_ROWS and (lt & (lt - 1)) == 0
    m = nb * lt
    (g_pre_mix, w_in, w_pool, pool_scale, g_hg, w_out, g_post_mix, g_pre_ffn,
     w_gate, w_up, w_down, g_post_ffn) = weights
    consts = (hist, s0, lb_logits, g_pre_mix, w_in, w_pool, pool_scale, g_hg, w_out,
              g_post_mix, g_pre_ffn, w_gate, w_up, w_down, g_post_ffn)
    x_spec = pl.BlockSpec((nb, lt, D_MODEL), lambda i: (0, i, 0))
    kern = functools.partial(_layer_kernel, nb=nb, lt=lt, n_valid=n_valid, layer_idx=layer_idx)
    return pl.pallas_call(
        kern,
        grid=(seq // lt,),
        in_specs=[x_spec] + [_const_spec(c.shape) for c in consts],
        out_specs=[x_spec, _const_spec((nb, POOL_HIST, D_POOL)),
                   _const_spec((nb, N_HEADS, HEAD_DIM, HEAD_DIM))],
        out_shape=[jax.ShapeDtypeStruct(x.shape, x.dtype),
                   jax.ShapeDtypeStruct((nb, POOL_HIST, D_POOL), x.dtype),
                   jax.ShapeDtypeStruct((nb, N_HEADS, HEAD_DIM, HEAD_DIM), jnp.float32)],
        scratch_shapes=[pltpu.VMEM((m, 4 * D_HG), jnp.float32),
                        pltpu.VMEM((nb, HIST_ROWS + lt, D_POOL), jnp.float32),
                        pltpu.VMEM((m, D_MODEL), jnp.bfloat16)],
        compiler_params=pltpu.CompilerParams(dimension_semantics=("arbitrary",),
                                             vmem_limit_bytes=VMEM_LIMIT_BYTES),
        name=f"layer_lt{lt}",
    )(x, *consts)


def kernel(x_prompt, x_sample, cache_pool, state_hgrn, g_pre_mix, w_in, w_pool, pool_scale,
           lb_logits, g_hg_norm, w_out, g_post_mix, g_pre_ffn, w_gate, w_up, w_down, g_post_ffn):
    depth = w_in.shape[0]
    xp, xs = x_prompt, x_sample
    pools_p, hgrn_p, pools_s, hgrn_s = [], [], [], []
    for l in range(depth):
        row = lambda a: a[l][None, :]
        weights = (row(g_pre_mix), _bf(w_in[l]), _bf(w_pool[l]), row(pool_scale), row(g_hg_norm),
                   _bf(w_out[l]), row(g_post_mix), row(g_pre_ffn), _bf(w_gate[l]), _bf(w_up[l]),
                   _bf(w_down[l]), row(g_post_ffn))
        hist0 = jnp.zeros((xp.shape[0], POOL_HIST, D_POOL), xp.dtype)
        s00 = jnp.zeros((xp.shape[0], N_HEADS, HEAD_DIM, HEAD_DIM), jnp.float32)
        xp, pool_p, s_p = _layer_call(xp, hist0, s00, lb_logits, weights,
                                      lt=PROMPT_CHUNK, n_valid=0, layer_idx=l)
        xs, pool_s, s_s = _layer_call(xs, cache_pool[l], state_hgrn[l], lb_logits, weights,
                                      lt=xs.shape[1], n_valid=POOL_HIST, layer_idx=l)
        pools_p.append(pool_p)
        hgrn_p.append(s_p)
        pools_s.append(pool_s)
        hgrn_s.append(s_s)
    return (xp, xs, jnp.stack(pools_p), jnp.stack(hgrn_p), jnp.stack(pools_s), jnp.stack(hgrn_s))
```

```python
import functools
import math

import jax
import jax.numpy as jnp
from jax import lax
from jax.experimental import pallas as pl
from jax.experimental.pallas import tpu as pltpu

D_MODEL = 1024
D_POOL = 512
D_HG = 512
POOL_WINDOWS = (2, 4, 8, 16)
POOL_GROUP = D_POOL // len(POOL_WINDOWS)
POOL_HIST = max(POOL_WINDOWS) - 1
HIST_ROWS = POOL_HIST + 1
HEAD_DIM = 128
N_HEADS = D_HG // HEAD_DIM
D_IN = D_POOL + 4 * D_HG
EPS = 1e-6
LOG2_E = math.log2(math.e)
PROMPT_CHUNK = 64
FFN_CHUNK = 1024

BASE_BLOCK = 16
MAX_FACTOR_LOG = 40.0
MIDPOINT_MIN_LB = math.exp(-MAX_FACTOR_LOG / (BASE_BLOCK // 2 - 1))

V7X_VMEM_BYTES = 64 * 1024 * 1024
VMEM_LIMIT_BYTES = V7X_VMEM_BYTES - 8 * 1024 * 1024

_NT = (((1,), (1,)), ((), ()))
_TN = (((0,), (0,)), ((), ()))


def _rms(x, g):
    return x * lax.rsqrt(jnp.mean(x * x, axis=-1, keepdims=True) + EPS) * g


def _silu(x):
    u = 0.5 * x
    return u + u * jnp.tanh(u)


def _bf(x):
    return x.astype(jnp.bfloat16)


def _dot(a, b, dims=None):
    if dims is None:
        return jnp.dot(a, b, preferred_element_type=jnp.float32)
    return lax.dot_general(a, b, dims, preferred_element_type=jnp.float32)


def _layer_kernel(x_ref, hist_ref, s0_ref, lbl_ref, g_pre_mix_ref, w_in_ref, w_pool_ref,
                  pool_scale_ref, g_hg_ref, w_out_ref, g_post_mix_ref, g_pre_ffn_ref,
                  w_gate_ref, w_up_ref, w_down_ref, g_post_ffn_ref,
                  y_ref, pool_out_ref, s_out_ref,
                  z_ref, ext_ref, ymix_ref,
                  *, nb, lt, n_valid, layer_idx):
    pid = pl.program_id(0)
    last = pl.num_programs(0) - 1
    m = nb * lt

    @pl.when(pid == 0)
    def _init():
        ext_ref[:, 0:1, :] = jnp.zeros((nb, 1, D_POOL), jnp.float32)
        ext_ref[:, 1:HIST_ROWS, :] = hist_ref[...]
        for i in range(nb):
            for h in range(N_HEADS):
                s_out_ref[i, h] = s0_ref[i, h].T

    lbl = lbl_ref[...]
    lbe = jnp.exp(lbl - jnp.max(lbl, axis=0, keepdims=True))
    lb = (jnp.sum(lbe[:layer_idx + 1], axis=0, keepdims=True)
          / jnp.sum(lbe, axis=0, keepdims=True))

    x = x_ref[...].reshape(m, D_MODEL)
    h = _bf(_rms(x, g_pre_mix_ref[...]))
    ext_ref[:, HIST_ROWS:HIST_ROWS + lt, :] = _dot(h, w_in_ref[:, 0:D_POOL]).reshape(nb, lt, D_POOL)
    z_ref[...] = _dot(h, w_in_ref[:, D_POOL:D_IN])

    t_loc = lax.broadcasted_iota(jnp.int32, (lt, 1), 0)
    pos = (pid * lt + t_loc + 1 + n_valid).astype(jnp.float32)
    inv_cnt = [1.0 / jnp.minimum(pos, float(w)) for w in POOL_WINDOWS]

    ti = lax.broadcasted_iota(jnp.int32, (lt, lt), 0)
    si = lax.broadcasted_iota(jnp.int32, (lt, lt), 1)
    pair_xor = jnp.where(ti > si, ti ^ si, 0)
    diag = ti == si
    in_base_block = (pair_xor < BASE_BLOCK) & (ti >= si)
    row = lax.broadcasted_iota(jnp.int32, (lt, HEAD_DIM), 0)
    g_hg = g_hg_ref[...]
    pool_scale = pool_scale_ref[...]

    def pair_scores(q, k, log2f, midpoint):
        attn = None if midpoint else jnp.where(diag, _dot(_bf(q), _bf(k), _NT), 0.0)
        c = log2f
        tot = log2f
        n = 1
        while n < lt:
            lower = (row & n) != 0
            if midpoint and 2 * n == BASE_BLOCK:
                a = jnp.where(lower, c, c - tot)
                p = _dot(_bf(q * jnp.exp2(a)), _bf(k * jnp.exp2(-a)), _NT)
                attn = jnp.where(in_base_block, p, 0.0)
            elif not midpoint or n >= BASE_BLOCK:
                p = _dot(_bf(q * jnp.exp2(c)), _bf(k * jnp.exp2(tot - c)), _NT)
                attn = jnp.where((pair_xor >= n) & (pair_xor < 2 * n), p, attn)
            dn = pltpu.roll(tot, n, 0)
            up = pltpu.roll(tot, lt - n, 0)
            c = c + jnp.where(lower, dn, 0.0)
            tot = tot + jnp.where(lower, dn, up)
            n *= 2
        return attn, c, tot

    def per_seq(b, midpoint):
        rows = pl.ds(pl.multiple_of(b * lt, lt), lt)

        for g, w in enumerate(POOL_WINDOWS):
            cols = slice(g * POOL_GROUP, (g + 1) * POOL_GROUP)
            e = ext_ref[b, :, cols]
            win = e
            span = 1
            while span < w:
                win = win + pltpu.roll(win, span, 0)
                span *= 2
            cur = e[HIST_ROWS:]
            d = win[HIST_ROWS:] * inv_cnt[g] - cur
            yp = _dot(_bf(d), w_pool_ref[g]) * pool_scale[:, cols]
            ymix_ref[rows, cols] = _bf(yp)
        ext_ref[b, 0:HIST_ROWS, :] = ext_ref[b, lt:lt + HIST_ROWS, :]

        for hd in range(N_HEADS):
            c0 = hd * HEAD_DIM
            zq = z_ref[rows, c0:c0 + HEAD_DIM]
            zf = z_ref[rows, D_HG + c0:D_HG + c0 + HEAD_DIM]
            v = z_ref[rows, 2 * D_HG + c0:2 * D_HG + c0 + HEAD_DIM]
            zg = z_ref[rows, 3 * D_HG + c0:3 * D_HG + c0 + HEAD_DIM]
            lbh = lb[:, c0:c0 + HEAD_DIM]

            q = _silu(zq)
            e = jnp.exp(-jnp.abs(zf))
            r = 1.0 / (1.0 + e)
            er = e * r
            pos_f = zf >= 0.0
            fgate = lbh + (1.0 - lbh) * jnp.where(pos_f, r, er)
            k = (1.0 - lbh) * jnp.where(pos_f, er, r)
            log2f = jnp.log(fgate) * LOG2_E

            attn, c, tot = pair_scores(q, k, log2f, midpoint)

            st = s_out_ref[b, hd]
            o = _dot(_bf(attn), _bf(v)) + _dot(_bf(q * jnp.exp2(c)), _bf(st), _NT)
            k_dec = k * jnp.exp2(tot - c)
            s_out_ref[b, hd] = st * jnp.exp2(tot[0:1, :]) + _dot(_bf(v), _bf(k_dec), _TN)

            o = _rms(o, g_hg) * _silu(zg)
            ymix_ref[rows, D_POOL + c0:D_POOL + c0 + HEAD_DIM] = _bf(o)

    def mixers(midpoint):
        lax.fori_loop(0, nb, lambda b, carry: per_seq(b, midpoint), None, unroll=4)

    lax.cond(jnp.min(lb) >= MIDPOINT_MIN_LB,
             functools.partial(mixers, True), functools.partial(mixers, False))

    @pl.when(pid == last)
    def _carry_out():
        pool_out_ref[...] = ext_ref[:, 1:HIST_ROWS, :]
        for i in range(nb):
            for hd in range(N_HEADS):
                s_out_ref[i, hd] = s_out_ref[i, hd].T

    x1 = x + _rms(_dot(ymix_ref[...], w_out_ref[...]), g_post_mix_ref[...])
    h2 = _bf(_rms(x1, g_pre_ffn_ref[...]))
    f = None
    for c0 in range(0, w_gate_ref.shape[1], FFN_CHUNK):
        cols = slice(c0, min(c0 + FFN_CHUNK, w_gate_ref.shape[1]))
        gate = _dot(h2, w_gate_ref[:, cols])
        upp = _dot(h2, w_up_ref[:, cols])
        act = _bf(_silu(gate) * upp)
        part = _dot(act, w_down_ref[cols, :])
        f = part if f is None else f + part
    y_ref[...] = (x1 + _rms(f, g_post_ffn_ref[...])).reshape(nb, lt, D_MODEL)


def _const_spec(shape):
    zeros = (0,) * len(shape)
    return pl.BlockSpec(shape, lambda i: zeros, pipeline_mode=pl.Buffered(1))


def _layer_call(x, hist, s0, lb_logits, weights, *, lt, n_valid, layer_idx):
    nb, seq, _ = x.shape
    assert seq % lt == 0 and lt % BASE_BLOCK == 0 and lt >= HIST_ROWS and (lt & (lt - 1)) == 0
    m = nb * lt
    (g_pre_mix, w_in, w_pool, pool_scale, g_hg, w_out, g_post_mix, g_pre_ffn,
     w_gate, w_up, w_down, g_post_ffn) = weights
    consts = (hist, s0, lb_logits, g_pre_mix, w_in, w_pool, pool_scale, g_hg, w_out,
              g_post_mix, g_pre_ffn, w_gate, w_up, w_down, g_post_ffn)
    x_spec = pl.BlockSpec((nb, lt, D_MODEL), lambda i: (0, i, 0))
    kern = functools.partial(_layer_kernel, nb=nb, lt=lt, n_valid=n_valid, layer_idx=layer_idx)
    return pl.pallas_call(
        kern,
        grid=(seq // lt,),
        in_specs=[x_spec] + [_const_spec(c.shape) for c in consts],
        out_specs=[x_spec, _const_spec((nb, POOL_HIST, D_POOL)),
                   _const_spec((nb, N_HEADS, HEAD_DIM, HEAD_DIM))],
        out_shape=[jax.ShapeDtypeStruct(x.shape, x.dtype),
                   jax.ShapeDtypeStruct((nb, POOL_HIST, D_POOL), x.dtype),
                   jax.ShapeDtypeStruct((nb, N_HEADS, HEAD_DIM, HEAD_DIM), jnp.float32)],
        scratch_shapes=[pltpu.VMEM((m, 4 * D_HG), jnp.float32),
                        pltpu.VMEM((nb, HIST_ROWS + lt, D_POOL), jnp.float32),
                        pltpu.VMEM((m, D_MODEL), jnp.bfloat16)],
        compiler_params=pltpu.CompilerParams(dimension_semantics=("arbitrary",),
                                             vmem_limit_bytes=VMEM_LIMIT_BYTES),
        name=f"layer_lt{lt}",
    )(x, *consts)


def kernel(x_prompt, x_sample, cache_pool, state_hgrn, g_pre_mix, w_in, w_pool, pool_scale,
           lb_logits, g_hg_norm, w_out, g_post_mix, g_pre_ffn, w_gate, w_up, w_down, g_post_ffn):
    depth = w_in.shape[0]
    xp, xs = x_prompt, x_sample
    pools_p, hgrn_p, pools_s, hgrn_s = [], [], [], []
    for l in range(depth):
        row = lambda a: a[l][None, :]
        weights = (row(g_pre_mix), _bf(w_in[l]), _bf(w_pool[l]), row(pool_scale), row(g_hg_norm),
                   _bf(w_out[l]), row(g_post_mix), row(g_pre_ffn), _bf(w_gate[l]), _bf(w_up[l]),
                   _bf(w_down[l]), row(g_post_ffn))
        hist0 = jnp.zeros((xp.shape[0], POOL_HIST, D_POOL), xp.dtype)
        s00 = jnp.zeros((xp.shape[0], N_HEADS, HEAD_DIM, HEAD_DIM), jnp.float32)
        xp, pool_p, s_p = _layer_call(xp, hist0, s00, lb_logits, weights,
                                      lt=PROMPT_CHUNK, n_valid=0, layer_idx=l)
        xs, pool_s, s_s = _layer_call(xs, cache_pool[l], state_hgrn[l], lb_logits, weights,
                                      lt=xs.shape[1], n_valid=POOL_HIST, layer_idx=l)
        pools_p.append(pool_p)
        hgrn_p.append(s_p)
        pools_s.append(pool_s)
        hgrn_s.append(s_s)
    return (xp, xs, jnp.stack(pools_p), jnp.stack(hgrn_p), jnp.stack(pools_s), jnp.stack(hgrn_s))
```

```python
import functools
import math

import jax
import jax.numpy as jnp
from jax import lax
from jax.experimental import pallas as pl
from jax.experimental.pallas import tpu as pltpu

D_MODEL = 1024
D_POOL = 512
D_HG = 512
POOL_WINDOWS = (2, 4, 8, 16)
POOL_GROUP = D_POOL // len(POOL_WINDOWS)
POOL_HIST = max(POOL_WINDOWS) - 1
HIST_ROWS = POOL_HIST + 1
HEAD_DIM = 128
N_HEADS = D_HG // HEAD_DIM
D_IN = D_POOL + 4 * D_HG
EPS = 1e-6
LOG2_E = math.log2(math.e)
PROMPT_CHUNK = 64
FFN_CHUNK = 1024
SEQ_GROUP = 4

BASE_BLOCK = 16
MAX_FACTOR_LOG = 40.0
MIDPOINT_MIN_LB = math.exp(-MAX_FACTOR_LOG / (BASE_BLOCK // 2 - 1))

V7X_VMEM_BYTES = 64 * 1024 * 1024
VMEM_LIMIT_BYTES = V7X_VMEM_BYTES - 8 * 1024 * 1024

_NT = (((1,), (1,)), ((), ()))
_TN = (((0,), (0,)), ((), ()))


def _rms(x, g):
    return x * lax.rsqrt(jnp.mean(x * x, axis=-1, keepdims=True) + EPS) * g


def _silu(x):
    u = 0.5 * x
    return u + u * jnp.tanh(u)


def _bf(x):
    return x.astype(jnp.bfloat16)


def _dot(a, b, dims=None):
    if dims is None:
        return jnp.dot(a, b, preferred_element_type=jnp.float32)
    return lax.dot_general(a, b, dims, preferred_element_type=jnp.float32)


def _layer_kernel(x_ref, hist_ref, s0_ref, lbl_ref, g_pre_mix_ref, w_in_ref, w_pool_ref,
                  pool_scale_ref, g_hg_ref, w_out_ref, g_post_mix_ref, g_pre_ffn_ref,
                  w_gate_ref, w_up_ref, w_down_ref, g_post_ffn_ref,
                  y_ref, pool_out_ref, s_out_ref,
                  z_ref, ext_ref, ymix_ref,
                  *, nb, lt, n_valid, layer_idx):
    pid = pl.program_id(0)
    last = pl.num_programs(0) - 1
    m = nb * lt

    @pl.when(pid == 0)
    def _init():
        ext_ref[:, 0:1, :] = jnp.zeros((nb, 1, D_POOL), jnp.float32)
        ext_ref[:, 1:HIST_ROWS, :] = hist_ref[...]
        for i in range(nb):
            for h in range(N_HEADS):
                s_out_ref[i, h] = s0_ref[i, h].T

    lbl = lbl_ref[...]
    lbe = jnp.exp(lbl - jnp.max(lbl, axis=0, keepdims=True))
    lb = (jnp.sum(lbe[:layer_idx + 1], axis=0, keepdims=True)
          / jnp.sum(lbe, axis=0, keepdims=True))

    x = x_ref[...].reshape(m, D_MODEL)
    h = _bf(_rms(x, g_pre_mix_ref[...]))
    ext_ref[:, HIST_ROWS:HIST_ROWS + lt, :] = _dot(h, w_in_ref[:, 0:D_POOL]).reshape(nb, lt, D_POOL)
    z_ref[...] = _dot(h, w_in_ref[:, D_POOL:D_IN])

    t_loc = lax.broadcasted_iota(jnp.int32, (lt, 1), 0)
    pos = (pid * lt + t_loc + 1 + n_valid).astype(jnp.float32)
    inv_cnt = [1.0 / jnp.minimum(pos, float(w)) for w in POOL_WINDOWS]

    ti = lax.broadcasted_iota(jnp.int32, (lt, lt), 0)
    si = lax.broadcasted_iota(jnp.int32, (lt, lt), 1)
    pair_xor = jnp.where(ti > si, ti ^ si, 0)
    diag = ti == si
    in_base_block = (pair_xor < BASE_BLOCK) & (ti >= si)
    row = lax.broadcasted_iota(jnp.int32, (lt, HEAD_DIM), 0)
    g_hg = g_hg_ref[...]
    pool_scale = pool_scale_ref[...]

    def seq_rows(b):
        return pl.ds(pl.multiple_of(b * lt, lt), lt)

    def pair_products(q, k, log2f, midpoint):
        prods = [] if midpoint else [("diag", _dot(_bf(q), _bf(k), _NT))]
        c = log2f
        tot = log2f
        n = 1
        while n < lt:
            lower = (row & n) != 0
            if midpoint and 2 * n == BASE_BLOCK:
                a = jnp.where(lower, c, c - tot)
                prods.append(("base", _dot(_bf(q * jnp.exp2(a)), _bf(k * jnp.exp2(-a)), _NT)))
            elif not midpoint or n >= BASE_BLOCK:
                prods.append((n, _dot(_bf(q * jnp.exp2(c)), _bf(k * jnp.exp2(tot - c)), _NT)))
            dn = pltpu.roll(tot, n, 0)
            up = pltpu.roll(tot, lt - n, 0)
            c = c + jnp.where(lower, dn, 0.0)
            tot = tot + jnp.where(lower, dn, up)
            n *= 2
        return prods, c, tot

    def pair_scores(prods):
        attn = None
        for which, p in prods:
            if which == "diag":
                attn = jnp.where(diag, p, 0.0)
            elif which == "base":
                attn = jnp.where(in_base_block, p, 0.0)
            else:
                attn = jnp.where((pair_xor >= which) & (pair_xor < 2 * which), p, attn)
        return attn

    def pool_mixer(b):
        rows = seq_rows(b)
        for g, w in enumerate(POOL_WINDOWS):
            cols = slice(g * POOL_GROUP, (g + 1) * POOL_GROUP)
            e = ext_ref[b, :, cols]
            win = e
            span = 1
            while span < w:
                win = win + pltpu.roll(win, span, 0)
                span *= 2
            cur = e[HIST_ROWS:]
            d = win[HIST_ROWS:] * inv_cnt[g] - cur
            yp = _dot(_bf(d), w_pool_ref[g]) * pool_scale[:, cols]
            ymix_ref[rows, cols] = _bf(yp)
        ext_ref[b, 0:HIST_ROWS, :] = ext_ref[b, lt:lt + HIST_ROWS, :]

    def head_products(b, hd, midpoint):
        rows = seq_rows(b)
        c0 = hd * HEAD_DIM
        zq = z_ref[rows, c0:c0 + HEAD_DIM]
        zf = z_ref[rows, D_HG + c0:D_HG + c0 + HEAD_DIM]
        v = _bf(z_ref[rows, 2 * D_HG + c0:2 * D_HG + c0 + HEAD_DIM])
        lbh = lb[:, c0:c0 + HEAD_DIM]

        q = _silu(zq)
        e = jnp.exp(-jnp.abs(zf))
        r = 1.0 / (1.0 + e)
        er = e * r
        pos_f = zf >= 0.0
        fgate = lbh + (1.0 - lbh) * jnp.where(pos_f, r, er)
        k = (1.0 - lbh) * jnp.where(pos_f, er, r)
        log2f = jnp.log(fgate) * LOG2_E

        prods, c, tot = pair_products(q, k, log2f, midpoint)

        st = s_out_ref[b, hd]
        o_state = _dot(_bf(q * jnp.exp2(c)), _bf(st), _NT)
        k_dec = _bf(k * jnp.exp2(tot - c))
        s_out_ref[b, hd] = st * jnp.exp2(tot[0:1, :]) + _dot(v, k_dec, _TN)
        return prods, v, o_state

    def head_output(b, hd, attn, v, o_state):
        rows = seq_rows(b)
        c0 = hd * HEAD_DIM
        zg = z_ref[rows, 3 * D_HG + c0:3 * D_HG + c0 + HEAD_DIM]
        o = _rms(_dot(_bf(attn), v) + o_state, g_hg) * _silu(zg)
        ymix_ref[rows, D_POOL + c0:D_POOL + c0 + HEAD_DIM] = _bf(o)

    def seq_group(i, midpoint):
        units = [(i * SEQ_GROUP + j, hd) for j in range(SEQ_GROUP) for hd in range(N_HEADS)]
        for j in range(SEQ_GROUP):
            pool_mixer(i * SEQ_GROUP + j)
        parts = [head_products(b, hd, midpoint) for b, hd in units]
        for (b, hd), (prods, v, o_state) in zip(units, parts):
            head_output(b, hd, pair_scores(prods), v, o_state)

    def mixers(midpoint):
        lax.fori_loop(0, nb // SEQ_GROUP, lambda i, carry: seq_group(i, midpoint), None)

    lax.cond(jnp.min(lb) >= MIDPOINT_MIN_LB,
             functools.partial(mixers, True), functools.partial(mixers, False))

    @pl.when(pid == last)
    def _carry_out():
        pool_out_ref[...] = ext_ref[:, 1:HIST_ROWS, :]
        for i in range(nb):
            for hd in range(N_HEADS):
                s_out_ref[i, hd] = s_out_ref[i, hd].T

    x1 = x + _rms(_dot(ymix_ref[...], w_out_ref[...]), g_post_mix_ref[...])
    h2 = _bf(_rms(x1, g_pre_ffn_ref[...]))
    f = None
    for c0 in range(0, w_gate_ref.shape[1], FFN_CHUNK):
        cols = slice(c0, min(c0 + FFN_CHUNK, w_gate_ref.shape[1]))
        gate = _dot(h2, w_gate_ref[:, cols])
        upp = _dot(h2, w_up_ref[:, cols])
        act = _bf(_silu(gate) * upp)
        part = _dot(act, w_down_ref[cols, :])
        f = part if f is None else f + part
    y_ref[...] = (x1 + _rms(f, g_post_ffn_ref[...])).reshape(nb, lt, D_MODEL)


def _const_spec(shape):
    zeros = (0,) * len(shape)
    return pl.BlockSpec(shape, lambda i: zeros, pipeline_mode=pl.Buffered(1))


def _layer_call(x, hist, s0, lb_logits, weights, *, lt, n_valid, layer_idx):
    nb, seq, _ = x.shape
    assert seq % lt == 0 and lt % BASE_BLOCK == 0 and lt >= HIST_ROWS and (lt & (lt - 1)) == 0
    assert nb % SEQ_GROUP == 0
    m = nb * lt
    (g_pre_mix, w_in, w_pool, pool_scale, g_hg, w_out, g_post_mix, g_pre_ffn,
     w_gate, w_up, w_down, g_post_ffn) = weights
    consts = (hist, s0, lb_logits, g_pre_mix, w_in, w_pool, pool_scale, g_hg, w_out,
              g_post_mix, g_pre_ffn, w_gate, w_up, w_down, g_post_ffn)
    x_spec = pl.BlockSpec((nb, lt, D_MODEL), lambda i: (0, i, 0))
    kern = functools.partial(_layer_kernel, nb=nb, lt=lt, n_valid=n_valid, layer_idx=layer_idx)
    return pl.pallas_call(
        kern,
        grid=(seq // lt,),
        in_specs=[x_spec] + [_const_spec(c.shape) for c in consts],
        out_specs=[x_spec, _const_spec((nb, POOL_HIST, D_POOL)),
                   _const_spec((nb, N_HEADS, HEAD_DIM, HEAD_DIM))],
        out_shape=[jax.ShapeDtypeStruct(x.shape, x.dtype),
                   jax.ShapeDtypeStruct((nb, POOL_HIST, D_POOL), x.dtype),
                   jax.ShapeDtypeStruct((nb, N_HEADS, HEAD_DIM, HEAD_DIM), jnp.float32)],
        scratch_shapes=[pltpu.VMEM((m, 4 * D_HG), jnp.float32),
                        pltpu.VMEM((nb, HIST_ROWS + lt, D_POOL), jnp.float32),
                        pltpu.VMEM((m, D_MODEL), jnp.bfloat16)],
        compiler_params=pltpu.CompilerParams(dimension_semantics=("arbitrary",),
                                             vmem_limit_bytes=VMEM_LIMIT_BYTES),
        name=f"layer_lt{lt}",
    )(x, *consts)


def kernel(x_prompt, x_sample, cache_pool, state_hgrn, g_pre_mix, w_in, w_pool, pool_scale,
           lb_logits, g_hg_norm, w_out, g_post_mix, g_pre_ffn, w_gate, w_up, w_down, g_post_ffn):
    depth = w_in.shape[0]
    xp, xs = x_prompt, x_sample
    pools_p, hgrn_p, pools_s, hgrn_s = [], [], [], []
    for l in range(depth):
        row = lambda a: a[l][None, :]
        weights = (row(g_pre_mix), _bf(w_in[l]), _bf(w_pool[l]), row(pool_scale), row(g_hg_norm),
                   _bf(w_out[l]), row(g_post_mix), row(g_pre_ffn), _bf(w_gate[l]), _bf(w_up[l]),
                   _bf(w_down[l]), row(g_post_ffn))
        hist0 = jnp.zeros((xp.shape[0], POOL_HIST, D_POOL), xp.dtype)
        s00 = jnp.zeros((xp.shape[0], N_HEADS, HEAD_DIM, HEAD_DIM), jnp.float32)
        xp, pool_p, s_p = _layer_call(xp, hist0, s00, lb_logits, weights,
                                      lt=PROMPT_CHUNK, n_valid=0, layer_idx=l)
        xs, pool_s, s_s = _layer_call(xs, cache_pool[l], state_hgrn[l], lb_logits, weights,
                                      lt=xs.shape[1], n_valid=POOL_HIST, layer_idx=l)
        pools_p.append(pool_p)
        hgrn_p.append(s_p)
        pools_s.append(pool_s)
        hgrn_s.append(s_s)
    return (xp, xs, jnp.stack(pools_p), jnp.stack(hgrn_p), jnp.stack(pools_s), jnp.stack(hgrn_s))
```

```python
import functools
import math

import jax
import jax.numpy as jnp
from jax import lax
from jax.experimental import pallas as pl
from jax.experimental.pallas import tpu as pltpu

D_MODEL = 1024
D_POOL = 512
D_HG = 512
POOL_WINDOWS = (2, 4, 8, 16)
POOL_GROUP = D_POOL // len(POOL_WINDOWS)
POOL_HIST = max(POOL_WINDOWS) - 1
HIST_ROWS = POOL_HIST + 1
HEAD_DIM = 128
N_HEADS = D_HG // HEAD_DIM
D_IN = D_POOL + 4 * D_HG
EPS = 1e-6
LOG2_E = math.log2(math.e)
PROMPT_CHUNK = 64
FFN_CHUNK = 1024
SEQ_GROUP = 4

BASE_BLOCK = 16
SUBLANES = 8
assert BASE_BLOCK == 2 * SUBLANES
MAX_FACTOR_LOG = 40.0
MIDPOINT_MIN_LB = math.exp(-MAX_FACTOR_LOG / (BASE_BLOCK // 2 - 1))

V7X_VMEM_BYTES = 64 * 1024 * 1024
VMEM_LIMIT_BYTES = V7X_VMEM_BYTES - 8 * 1024 * 1024

_NT = (((1,), (1,)), ((), ()))
_TN = (((0,), (0,)), ((), ()))


def _rms(x, g):
    return x * lax.rsqrt(jnp.mean(x * x, axis=-1, keepdims=True) + EPS) * g


def _rms_split(x, g):
    return x * g, lax.rsqrt(jnp.mean(x * x, axis=-1, keepdims=True) + EPS)


def _silu(x):
    u = 0.5 * x
    return u + u * jnp.tanh(u)


def _bf(x):
    return x.astype(jnp.bfloat16)


def _dot(a, b, dims=None):
    if dims is None:
        return jnp.dot(a, b, preferred_element_type=jnp.float32)
    return lax.dot_general(a, b, dims, preferred_element_type=jnp.float32)


def _layer_kernel(x_ref, hist_ref, s0_ref, lbl_ref, g_pre_mix_ref, w_in_ref, w_pool_ref,
                  pool_scale_ref, g_hg_ref, w_out_ref, g_post_mix_ref, g_pre_ffn_ref,
                  w_gate_ref, w_up_ref, w_down_ref, g_post_ffn_ref,
                  y_ref, pool_out_ref, s_out_ref,
                  z_ref, ext_ref, ymix_ref,
                  *, nb, lt, n_valid, layer_idx):
    pid = pl.program_id(0)
    last = pl.num_programs(0) - 1
    m = nb * lt

    @pl.when(pid == 0)
    def _init():
        ext_ref[:, 0:1, :] = jnp.zeros((nb, 1, D_POOL), jnp.float32)
        ext_ref[:, 1:HIST_ROWS, :] = hist_ref[...]
        for i in range(nb):
            for h in range(N_HEADS):
                s_out_ref[i, h] = s0_ref[i, h].T

    lbl = lbl_ref[...]
    lbe = jnp.exp(lbl - jnp.max(lbl, axis=0, keepdims=True))
    lb = (jnp.sum(lbe[:layer_idx + 1], axis=0, keepdims=True)
          / jnp.sum(lbe, axis=0, keepdims=True))

    x = x_ref[...].reshape(m, D_MODEL)
    xg, x_scale = _rms_split(x, g_pre_mix_ref[...])
    h = _bf(xg)
    ext_ref[:, HIST_ROWS:HIST_ROWS + lt, :] = (
        _dot(h, w_in_ref[:, 0:D_POOL]) * x_scale).reshape(nb, lt, D_POOL)
    z_ref[...] = _dot(h, w_in_ref[:, D_POOL:D_IN]) * x_scale

    t_loc = lax.broadcasted_iota(jnp.int32, (lt, 1), 0)
    pos = (pid * lt + t_loc + 1 + n_valid).astype(jnp.float32)
    inv_cnt = [1.0 / jnp.minimum(pos, float(w)) for w in POOL_WINDOWS]

    ti = lax.broadcasted_iota(jnp.int32, (lt, lt), 0)
    si = lax.broadcasted_iota(jnp.int32, (lt, lt), 1)
    pair_xor = jnp.where(ti > si, ti ^ si, 0)
    diag = ti == si
    in_base_block = (pair_xor < BASE_BLOCK) & (ti >= si)
    row = lax.broadcasted_iota(jnp.int32, (lt, HEAD_DIM), 0)
    g_hg = g_hg_ref[...]
    pool_scale = pool_scale_ref[...]

    def seq_rows(b):
        return pl.ds(pl.multiple_of(b * lt, lt), lt)

    def half_block_scan(log2f):
        t = log2f.reshape(lt // SUBLANES, SUBLANES, HEAD_DIM)
        sub = lax.broadcasted_iota(jnp.int32, t.shape, 1)
        s = 1
        while s < SUBLANES:
            t = t + jnp.where(sub >= s, pltpu.roll(t, s, 1), 0.0)
            s *= 2
        tot = jnp.broadcast_to(t[:, SUBLANES - 1:SUBLANES, :], t.shape)
        return t.reshape(lt, HEAD_DIM), tot.reshape(lt, HEAD_DIM)

    def pair_products(q, k, log2f, midpoint):
        if midpoint:
            prods = []
            c, tot = half_block_scan(log2f)
            n = BASE_BLOCK // 2
        else:
            prods = [("diag", _dot(_bf(q), _bf(k), _NT))]
            c = log2f
            tot = log2f
            n = 1
        while n < lt:
            lower = (row & n) != 0
            if midpoint and 2 * n == BASE_BLOCK:
                a = jnp.where(lower, c, c - tot)
                prods.append(("base", _dot(_bf(q * jnp.exp2(a)), _bf(k * jnp.exp2(-a)), _NT)))
            elif not midpoint or n >= BASE_BLOCK:
                prods.append((n, _dot(_bf(q * jnp.exp2(c)), _bf(k * jnp.exp2(tot - c)), _NT)))
            dn = pltpu.roll(tot, n, 0)
            up = pltpu.roll(tot, lt - n, 0)
            c = c + jnp.where(lower, dn, 0.0)
            tot = tot + jnp.where(lower, dn, up)
            n *= 2
        return prods, c, tot

    def pair_scores(prods):
        attn = None
        for which, p in prods:
            if which == "diag":
                attn = jnp.where(diag, p, 0.0)
            elif which == "base":
                attn = jnp.where(in_base_block, p, 0.0)
            else:
                attn = jnp.where((pair_xor >= which) & (pair_xor < 2 * which), p, attn)
        return attn

    def pool_mixer(b):
        rows = seq_rows(b)
        for g, w in enumerate(POOL_WINDOWS):
            cols = slice(g * POOL_GROUP, (g + 1) * POOL_GROUP)
            e = ext_ref[b, :, cols]
            win = e
            span = 1
            while span < w:
                win = win + pltpu.roll(win, span, 0)
                span *= 2
            cur = e[HIST_ROWS:]
            d = win[HIST_ROWS:] * inv_cnt[g] - cur
            yp = _dot(_bf(d), w_pool_ref[g]) * pool_scale[:, cols]
            ymix_ref[rows, cols] = _bf(yp)
        ext_ref[b, 0:HIST_ROWS, :] = ext_ref[b, lt:lt + HIST_ROWS, :]

    def head_products(b, hd, midpoint):
        rows = seq_rows(b)
        c0 = hd * HEAD_DIM
        zq = z_ref[rows, c0:c0 + HEAD_DIM]
        zf = z_ref[rows, D_HG + c0:D_HG + c0 + HEAD_DIM]
        v = _bf(z_ref[rows, 2 * D_HG + c0:2 * D_HG + c0 + HEAD_DIM])
        lbh = lb[:, c0:c0 + HEAD_DIM]

        q = _silu(zq)
        fgate = lbh + (1.0 - lbh) * (0.5 + 0.5 * jnp.tanh(0.5 * zf))
        k = 1.0 - fgate
        log2f = jnp.log(fgate) * LOG2_E

        prods, c, tot = pair_products(q, k, log2f, midpoint)

        st = s_out_ref[b, hd]
        o_state = _dot(_bf(q * jnp.exp2(c)), _bf(st), _NT)
        k_dec = _bf(k * jnp.exp2(tot - c))
        s_out_ref[b, hd] = st * jnp.exp2(tot[0:1, :]) + _dot(v, k_dec, _TN)
        return prods, v, o_state

    def head_output(b, hd, attn, v, o_state):
        rows = seq_rows(b)
        c0 = hd * HEAD_DIM
        zg = z_ref[rows, 3 * D_HG + c0:3 * D_HG + c0 + HEAD_DIM]
        o = _rms(_dot(_bf(attn), v) + o_state, g_hg) * _silu(zg)
        ymix_ref[rows, D_POOL + c0:D_POOL + c0 + HEAD_DIM] = _bf(o)

    def seq_group(i, midpoint):
        units = [(i * SEQ_GROUP + j, hd) for j in range(SEQ_GROUP) for hd in range(N_HEADS)]
        for j in range(SEQ_GROUP):
            pool_mixer(i * SEQ_GROUP + j)
        parts = [head_products(b, hd, midpoint) for b, hd in units]
        for (b, hd), (prods, v, o_state) in zip(units, parts):
            head_output(b, hd, pair_scores(prods), v, o_state)

    def mixers(midpoint):
        lax.fori_loop(0, nb // SEQ_GROUP, lambda i, carry: seq_group(i, midpoint), None)

    lax.cond(jnp.min(lb) >= MIDPOINT_MIN_LB,
             functools.partial(mixers, True), functools.partial(mixers, False))

    @pl.when(pid == last)
    def _carry_out():
        pool_out_ref[...] = ext_ref[:, 1:HIST_ROWS, :]
        for i in range(nb):
            for hd in range(N_HEADS):
                s_out_ref[i, hd] = s_out_ref[i, hd].T

    x1 = x + _rms(_dot(ymix_ref[...], w_out_ref[...]), g_post_mix_ref[...])
    x1g, x1_scale = _rms_split(x1, g_pre_ffn_ref[...])
    h2 = _bf(x1g)
    f = None
    for c0 in range(0, w_gate_ref.shape[1], FFN_CHUNK):
        cols = slice(c0, min(c0 + FFN_CHUNK, w_gate_ref.shape[1]))
        gate = _dot(h2, w_gate_ref[:, cols]) * x1_scale
        upp = _dot(h2, w_up_ref[:, cols]) * x1_scale
        act = _bf(_silu(gate) * upp)
        part = _dot(act, w_down_ref[cols, :])
        f = part if f is None else f + part
    y_ref[...] = (x1 + _rms(f, g_post_ffn_ref[...])).reshape(nb, lt, D_MODEL)


def _const_spec(shape):
    zeros = (0,) * len(shape)
    return pl.BlockSpec(shape, lambda i: zeros, pipeline_mode=pl.Buffered(1))


def _layer_call(x, hist, s0, lb_logits, weights, *, lt, n_valid, layer_idx):
    nb, seq, _ = x.shape
    assert seq % lt == 0 and lt % BASE_BLOCK == 0 and lt >= HIST_ROWS and (lt & (lt - 1)) == 0
    assert nb % SEQ_GROUP == 0
    m = nb * lt
    (g_pre_mix, w_in, w_pool, pool_scale, g_hg, w_out, g_post_mix, g_pre_ffn,
     w_gate, w_up, w_down, g_post_ffn) = weights
    consts = (hist, s0, lb_logits, g_pre_mix, w_in, w_pool, pool_scale, g_hg, w_out,
              g_post_mix, g_pre_ffn, w_gate, w_up, w_down, g_post_ffn)
    x_spec = pl.BlockSpec((nb, lt, D_MODEL), lambda i: (0, i, 0))
    kern = functools.partial(_layer_kernel, nb=nb, lt=lt, n_valid=n_valid, layer_idx=layer_idx)
    return pl.pallas_call(
        kern,
        grid=(seq // lt,),
        in_specs=[x_spec] + [_const_spec(c.shape) for c in consts],
        out_specs=[x_spec, _const_spec((nb, POOL_HIST, D_POOL)),
                   _const_spec((nb, N_HEADS, HEAD_DIM, HEAD_DIM))],
        out_shape=[jax.ShapeDtypeStruct(x.shape, x.dtype),
                   jax.ShapeDtypeStruct((nb, POOL_HIST, D_POOL), x.dtype),
                   jax.ShapeDtypeStruct((nb, N_HEADS, HEAD_DIM, HEAD_DIM), jnp.float32)],
        scratch_shapes=[pltpu.VMEM((m, 4 * D_HG), jnp.float32),
                        pltpu.VMEM((nb, HIST_ROWS + lt, D_POOL), jnp.float32),
                        pltpu.VMEM((m, D_MODEL), jnp.bfloat16)],
        compiler_params=pltpu.CompilerParams(dimension_semantics=("arbitrary",),
                                             vmem_limit_bytes=VMEM_LIMIT_BYTES),
        name=f"layer_lt{lt}",
    )(x, *consts)


def kernel(x_prompt, x_sample, cache_pool, state_hgrn, g_pre_mix, w_in, w_pool, pool_scale,
           lb_logits, g_hg_norm, w_out, g_post_mix, g_pre_ffn, w_gate, w_up, w_down, g_post_ffn):
    depth = w_in.shape[0]
    xp, xs = x_prompt, x_sample
    pools_p, hgrn_p, pools_s, hgrn_s = [], [], [], []
    for l in range(depth):
        row = lambda a: a[l][None, :]
        weights = (row(g_pre_mix), _bf(w_in[l]), _bf(w_pool[l]), row(pool_scale), row(g_hg_norm),
                   _bf(w_out[l]), row(g_post_mix), row(g_pre_ffn), _bf(w_gate[l]), _bf(w_up[l]),
                   _bf(w_down[l]), row(g_post_ffn))
        hist0 = jnp.zeros((xp.shape[0], POOL_HIST, D_POOL), xp.dtype)
        s00 = jnp.zeros((xp.shape[0], N_HEADS, HEAD_DIM, HEAD_DIM), jnp.float32)
        xp, pool_p, s_p = _layer_call(xp, hist0, s00, lb_logits, weights,
                                      lt=PROMPT_CHUNK, n_valid=0, layer_idx=l)
        xs, pool_s, s_s = _layer_call(xs, cache_pool[l], state_hgrn[l], lb_logits, weights,
                                      lt=xs.shape[1], n_valid=POOL_HIST, layer_idx=l)
        pools_p.append(pool_p)
        hgrn_p.append(s_p)
        pools_s.append(pool_s)
        hgrn_s.append(s_s)
    return (xp, xs, jnp.stack(pools_p), jnp.stack(hgrn_p), jnp.stack(pools_s), jnp.stack(hgrn_s))
```

```python
import functools
import math

import jax
import jax.numpy as jnp
from jax import lax
from jax.experimental import pallas as pl
from jax.experimental.pallas import tpu as pltpu

D_MODEL = 1024
D_POOL = 512
D_HG = 512
POOL_WINDOWS = (2, 4, 8, 16)
POOL_GROUP = D_POOL // len(POOL_WINDOWS)
POOL_HIST = max(POOL_WINDOWS) - 1
HIST_ROWS = POOL_HIST + 1
HEAD_DIM = 128
N_HEADS = D_HG // HEAD_DIM
D_IN = D_POOL + 4 * D_HG
EPS = 1e-6
LOG2_E = math.log2(math.e)
PROMPT_CHUNK = 64
FFN_CHUNK = 1024
SEQ_GROUP = 4
ROW_GROUPS = 2

BASE_BLOCK = 16
SUBLANES = 8
assert BASE_BLOCK == 2 * SUBLANES
MAX_FACTOR_LOG = 40.0
MIDPOINT_MIN_LB = math.exp(-MAX_FACTOR_LOG / (BASE_BLOCK // 2 - 1))

V7X_VMEM_BYTES = 64 * 1024 * 1024
VMEM_LIMIT_BYTES = V7X_VMEM_BYTES - 8 * 1024 * 1024

_NT = (((1,), (1,)), ((), ()))
_TN = (((0,), (0,)), ((), ()))


def _rms(x, g):
    return x * lax.rsqrt(jnp.mean(x * x, axis=-1, keepdims=True) + EPS) * g


def _rms_split(x, g):
    return x * g, lax.rsqrt(jnp.mean(x * x, axis=-1, keepdims=True) + EPS)


def _silu(x):
    u = 0.5 * x
    return u + u * jnp.tanh(u)


def _bf(x):
    return x.astype(jnp.bfloat16)


def _dot(a, b, dims=None):
    if dims is None:
        return jnp.dot(a, b, preferred_element_type=jnp.float32)
    return lax.dot_general(a, b, dims, preferred_element_type=jnp.float32)


def _layer_kernel(x_ref, hist_ref, s0_ref, lbl_ref, g_pre_mix_ref, w_in_ref, w_pool_ref,
                  pool_scale_ref, g_hg_ref, w_out_ref, g_post_mix_ref, g_pre_ffn_ref,
                  w_gate_ref, w_up_ref, w_down_ref, g_post_ffn_ref,
                  y_ref, pool_out_ref, s_out_ref,
                  z_ref, ext_ref, ymix_ref,
                  *, nb, lt, n_valid, layer_idx):
    pid = pl.program_id(0)
    last = pl.num_programs(0) - 1
    m = nb * lt

    @pl.when(pid == 0)
    def _init():
        ext_ref[:, 0:1, :] = jnp.zeros((nb, 1, D_POOL), jnp.float32)
        ext_ref[:, 1:HIST_ROWS, :] = hist_ref[...]
        for i in range(nb):
            for h in range(N_HEADS):
                s_out_ref[i, h] = s0_ref[i, h].T

    lbl = lbl_ref[...]
    lbe = jnp.exp(lbl - jnp.max(lbl, axis=0, keepdims=True))
    lb = (jnp.sum(lbe[:layer_idx + 1], axis=0, keepdims=True)
          / jnp.sum(lbe, axis=0, keepdims=True))

    xg, x_scale = _rms_split(x_ref[...].reshape(m, D_MODEL), g_pre_mix_ref[...])
    h = _bf(xg)
    ext_ref[:, HIST_ROWS:HIST_ROWS + lt, :] = (
        _dot(h, w_in_ref[:, 0:D_POOL]) * x_scale).reshape(nb, lt, D_POOL)

    t_loc = lax.broadcasted_iota(jnp.int32, (lt, 1), 0)
    pos = (pid * lt + t_loc + 1 + n_valid).astype(jnp.float32)
    inv_cnt = [1.0 / jnp.minimum(pos, float(w)) for w in POOL_WINDOWS]

    ti = lax.broadcasted_iota(jnp.int32, (lt, lt), 0)
    si = lax.broadcasted_iota(jnp.int32, (lt, lt), 1)
    pair_xor = jnp.where(ti > si, ti ^ si, 0)
    diag = ti == si
    in_base_block = (pair_xor < BASE_BLOCK) & (ti >= si)
    row = lax.broadcasted_iota(jnp.int32, (lt, HEAD_DIM), 0)
    g_hg = g_hg_ref[...]
    pool_scale = pool_scale_ref[...]

    def seq_rows(b):
        return pl.ds(b * lt if isinstance(b, int) else pl.multiple_of(b * lt, lt), lt)

    def half_block_scan(log2f):
        t = log2f.reshape(lt // SUBLANES, SUBLANES, HEAD_DIM)
        sub = lax.broadcasted_iota(jnp.int32, t.shape, 1)
        s = 1
        while s < SUBLANES:
            t = t + jnp.where(sub >= s, pltpu.roll(t, s, 1), 0.0)
            s *= 2
        tot = jnp.broadcast_to(t[:, SUBLANES - 1:SUBLANES, :], t.shape)
        return t.reshape(lt, HEAD_DIM), tot.reshape(lt, HEAD_DIM)

    def pair_products(q, k, log2f, midpoint):
        if midpoint:
            prods = []
            c, tot = half_block_scan(log2f)
            n = BASE_BLOCK // 2
        else:
            prods = [("diag", _dot(_bf(q), _bf(k), _NT))]
            c = log2f
            tot = log2f
            n = 1
        while n < lt:
            lower = (row & n) != 0
            if midpoint and 2 * n == BASE_BLOCK:
                a = jnp.where(lower, c, c - tot)
                prods.append(("base", _dot(_bf(q * jnp.exp2(a)), _bf(k * jnp.exp2(-a)), _NT)))
            elif not midpoint or n >= BASE_BLOCK:
                prods.append((n, _dot(_bf(q * jnp.exp2(c)), _bf(k * jnp.exp2(tot - c)), _NT)))
            dn = pltpu.roll(tot, n, 0)
            up = pltpu.roll(tot, lt - n, 0)
            c = c + jnp.where(lower, dn, 0.0)
            tot = tot + jnp.where(lower, dn, up)
            n *= 2
        return prods, c, tot

    def pair_scores(prods):
        attn = None
        for which, p in prods:
            if which == "diag":
                attn = jnp.where(diag, p, 0.0)
            elif which == "base":
                attn = jnp.where(in_base_block, p, 0.0)
            else:
                attn = jnp.where((pair_xor >= which) & (pair_xor < 2 * which), p, attn)
        return attn

    def pool_mixer(b):
        rows = seq_rows(b)
        for g, w in enumerate(POOL_WINDOWS):
            cols = slice(g * POOL_GROUP, (g + 1) * POOL_GROUP)
            e = ext_ref[b, :, cols]
            win = e
            span = 1
            while span < w:
                win = win + pltpu.roll(win, span, 0)
                span *= 2
            cur = e[HIST_ROWS:]
            d = win[HIST_ROWS:] * inv_cnt[g] - cur
            yp = _dot(_bf(d), w_pool_ref[g]) * pool_scale[:, cols]
            ymix_ref[rows, cols] = _bf(yp)
        ext_ref[b, 0:HIST_ROWS, :] = ext_ref[b, lt:lt + HIST_ROWS, :]

    def head_products(b, hd, midpoint):
        rows = seq_rows(b)
        c0 = hd * HEAD_DIM
        zq = z_ref[rows, c0:c0 + HEAD_DIM]
        zf = z_ref[rows, D_HG + c0:D_HG + c0 + HEAD_DIM]
        v = _bf(z_ref[rows, 2 * D_HG + c0:2 * D_HG + c0 + HEAD_DIM])
        lbh = lb[:, c0:c0 + HEAD_DIM]

        q = _silu(zq)
        fgate = lbh + (1.0 - lbh) * (0.5 + 0.5 * jnp.tanh(0.5 * zf))
        k = 1.0 - fgate
        log2f = jnp.log(fgate) * LOG2_E

        prods, c, tot = pair_products(q, k, log2f, midpoint)

        st = s_out_ref[b, hd]
        o_state = _dot(_bf(q * jnp.exp2(c)), _bf(st), _NT)
        k_dec = _bf(k * jnp.exp2(tot - c))
        s_out_ref[b, hd] = st * jnp.exp2(tot[0:1, :]) + _dot(v, k_dec, _TN)
        return prods, v, o_state

    def head_output(b, hd, attn, v, o_state):
        rows = seq_rows(b)
        c0 = hd * HEAD_DIM
        zg = z_ref[rows, 3 * D_HG + c0:3 * D_HG + c0 + HEAD_DIM]
        o = _rms(_dot(_bf(attn), v) + o_state, g_hg) * _silu(zg)
        ymix_ref[rows, D_POOL + c0:D_POOL + c0 + HEAD_DIM] = _bf(o)

    def seq_group(i, midpoint):
        units = [(i * SEQ_GROUP + j, hd) for j in range(SEQ_GROUP) for hd in range(N_HEADS)]
        parts = [head_products(b, hd, midpoint) for b, hd in units]
        for (b, hd), (prods, v, o_state) in zip(units, parts):
            head_output(b, hd, pair_scores(prods), v, o_state)

    def mixers(midpoint):
        lax.fori_loop(0, nb // SEQ_GROUP, lambda i, carry: seq_group(i, midpoint), None)

    z_ref[...] = _dot(h, w_in_ref[:, D_POOL:D_IN]) * x_scale
    for b in range(nb):
        pool_mixer(b)

    lax.cond(jnp.min(lb) >= MIDPOINT_MIN_LB,
             functools.partial(mixers, True), functools.partial(mixers, False))

    @pl.when(pid == last)
    def _carry_out():
        pool_out_ref[...] = ext_ref[:, 1:HIST_ROWS, :]
        for i in range(nb):
            for hd in range(N_HEADS):
                s_out_ref[i, hd] = s_out_ref[i, hd].T

    groups = range(ROW_GROUPS)
    nb_g = nb // ROW_GROUPS
    m_g = nb_g * lt
    yproj = [_dot(ymix_ref[gi * m_g:(gi + 1) * m_g, :], w_out_ref[...]) for gi in groups]
    x1 = [x_ref[gi * nb_g:(gi + 1) * nb_g].reshape(m_g, D_MODEL)
          + _rms(yproj[gi], g_post_mix_ref[...]) for gi in groups]
    split = [_rms_split(x1[gi], g_pre_ffn_ref[...]) for gi in groups]
    h2 = [_bf(split[gi][0]) for gi in groups]
    f = [None for _ in groups]
    for c0 in range(0, w_gate_ref.shape[1], FFN_CHUNK):
        cols = slice(c0, min(c0 + FFN_CHUNK, w_gate_ref.shape[1]))
        for gi in groups:
            gate = _dot(h2[gi], w_gate_ref[:, cols]) * split[gi][1]
            upp = _dot(h2[gi], w_up_ref[:, cols]) * split[gi][1]
            act = _bf(_silu(gate) * upp)
            part = _dot(act, w_down_ref[cols, :])
            f[gi] = part if f[gi] is None else f[gi] + part
    for gi in groups:
        y_ref[gi * nb_g:(gi + 1) * nb_g] = (
            x1[gi] + _rms(f[gi], g_post_ffn_ref[...])).reshape(nb_g, lt, D_MODEL)


def _const_spec(shape):
    zeros = (0,) * len(shape)
    return pl.BlockSpec(shape, lambda i: zeros, pipeline_mode=pl.Buffered(1))


def _layer_call(x, hist, s0, lb_logits, weights, *, lt, n_valid, layer_idx):
    nb, seq, _ = x.shape
    assert seq % lt == 0 and lt % BASE_BLOCK == 0 and lt >= HIST_ROWS and (lt & (lt - 1)) == 0
    assert nb % SEQ_GROUP == 0 and nb % ROW_GROUPS == 0
    m = nb * lt
    (g_pre_mix, w_in, w_pool, pool_scale, g_hg, w_out, g_post_mix, g_pre_ffn,
     w_gate, w_up, w_down, g_post_ffn) = weights
    consts = (hist, s0, lb_logits, g_pre_mix, w_in, w_pool, pool_scale, g_hg, w_out,
              g_post_mix, g_pre_ffn, w_gate, w_up, w_down, g_post_ffn)
    x_spec = pl.BlockSpec((nb, lt, D_MODEL), lambda i: (0, i, 0))
    kern = functools.partial(_layer_kernel, nb=nb, lt=lt, n_valid=n_valid, layer_idx=layer_idx)
    return pl.pallas_call(
        kern,
        grid=(seq // lt,),
        in_specs=[x_spec] + [_const_spec(c.shape) for c in consts],
        out_specs=[x_spec, _const_spec((nb, POOL_HIST, D_POOL)),
                   _const_spec((nb, N_HEADS, HEAD_DIM, HEAD_DIM))],
        out_shape=[jax.ShapeDtypeStruct(x.shape, x.dtype),
                   jax.ShapeDtypeStruct((nb, POOL_HIST, D_POOL), x.dtype),
                   jax.ShapeDtypeStruct((nb, N_HEADS, HEAD_DIM, HEAD_DIM), jnp.float32)],
        scratch_shapes=[pltpu.VMEM((m, 4 * D_HG), jnp.float32),
                        pltpu.VMEM((nb, HIST_ROWS + lt, D_POOL), jnp.float32),
                        pltpu.VMEM((m, D_MODEL), jnp.bfloat16)],
        compiler_params=pltpu.CompilerParams(dimension_semantics=("arbitrary",),
                                             vmem_limit_bytes=VMEM_LIMIT_BYTES),
        name=f"layer_lt{lt}",
    )(x, *consts)


def kernel(x_prompt, x_sample, cache_pool, state_hgrn, g_pre_mix, w_in, w_pool, pool_scale,
           lb_logits, g_hg_norm, w_out, g_post_mix, g_pre_ffn, w_gate, w_up, w_down, g_post_ffn):
    depth = w_in.shape[0]
    xp, xs = x_prompt, x_sample
    pools_p, hgrn_p, pools_s, hgrn_s = [], [], [], []
    for l in range(depth):
        row = lambda a: a[l][None, :]
        weights = (row(g_pre_mix), _bf(w_in[l]), _bf(w_pool[l]), row(pool_scale), row(g_hg_norm),
                   _bf(w_out[l]), row(g_post_mix), row(g_pre_ffn), _bf(w_gate[l]), _bf(w_up[l]),
                   _bf(w_down[l]), row(g_post_ffn))
        hist0 = jnp.zeros((xp.shape[0], POOL_HIST, D_POOL), xp.dtype)
        s00 = jnp.zeros((xp.shape[0], N_HEADS, HEAD_DIM, HEAD_DIM), jnp.float32)
        xp, pool_p, s_p = _layer_call(xp, hist0, s00, lb_logits, weights,
                                      lt=PROMPT_CHUNK, n_valid=0, layer_idx=l)
        xs, pool_s, s_s = _layer_call(xs, cache_pool[l], state_hgrn[l], lb_logits, weights,
                                      lt=xs.shape[1], n_valid=POOL_HIST, layer_idx=l)
        pools_p.append(pool_p)
        hgrn_p.append(s_p)
        pools_s.append(pool_s)
        hgrn_s.append(s_s)
    return (xp, xs, jnp.stack(pools_p), jnp.stack(hgrn_p), jnp.stack(pools_s), jnp.stack(hgrn_s))
```

```python
import functools
import math

import jax
import jax.numpy as jnp
from jax import lax
from jax.experimental import pallas as pl
from jax.experimental.pallas import tpu as pltpu

D_MODEL = 1024
D_POOL = 512
D_HG = 512
POOL_WINDOWS = (2, 4, 8, 16)
POOL_GROUP = D_POOL // len(POOL_WINDOWS)
POOL_HIST = max(POOL_WINDOWS) - 1
HIST_ROWS = POOL_HIST + 1
HEAD_DIM = 128
N_HEADS = D_HG // HEAD_DIM
D_IN = D_POOL + 4 * D_HG
EPS = 1e-6
LOG2_E = math.log2(math.e)
PROMPT_CHUNK = 64
FFN_CHUNK = 1024
SEQ_GROUP = 4

BASE_BLOCK = 16
SUBLANES = 8
assert BASE_BLOCK == 2 * SUBLANES
MAX_FACTOR_LOG = 40.0
MIDPOINT_MIN_LB = math.exp(-MAX_FACTOR_LOG / (BASE_BLOCK // 2 - 1))

V7X_VMEM_BYTES = 64 * 1024 * 1024
VMEM_LIMIT_BYTES = V7X_VMEM_BYTES - 8 * 1024 * 1024

_NT = (((1,), (1,)), ((), ()))
_TN = (((0,), (0,)), ((), ()))


def _rms(x, g):
    return x * lax.rsqrt(jnp.mean(x * x, axis=-1, keepdims=True) + EPS) * g


def _rms_split(x, g):
    return x * g, lax.rsqrt(jnp.mean(x * x, axis=-1, keepdims=True) + EPS)


def _silu(x):
    u = 0.5 * x
    return u + u * jnp.tanh(u)


def _bf(x):
    return x.astype(jnp.bfloat16)


def _dot(a, b, dims=None):
    if dims is None:
        return jnp.dot(a, b, preferred_element_type=jnp.float32)
    return lax.dot_general(a, b, dims, preferred_element_type=jnp.float32)


def _layer_kernel(x_ref, hist_ref, s0_ref, lbl_ref, g_pre_mix_ref, w_in_ref, w_pool_ref,
                  pool_scale_ref, g_hg_ref, w_out_ref, g_post_mix_ref, g_pre_ffn_ref,
                  w_gate_ref, w_up_ref, w_down_ref, g_post_ffn_ref,
                  y_ref, pool_out_ref, s_out_ref,
                  z_ref, ext_ref, ymix_ref,
                  *, nb, lt, n_valid, layer_idx):
    pid = pl.program_id(0)
    last = pl.num_programs(0) - 1
    m = nb * lt

    @pl.when(pid == 0)
    def _init():
        ext_ref[:, 0:1, :] = jnp.zeros((nb, 1, D_POOL), jnp.float32)
        ext_ref[:, 1:HIST_ROWS, :] = hist_ref[...]
        for i in range(nb):
            for h in range(N_HEADS):
                s_out_ref[i, h] = s0_ref[i, h].T

    lbl = lbl_ref[...]
    lbe = jnp.exp(lbl - jnp.max(lbl, axis=0, keepdims=True))
    lb = (jnp.sum(lbe[:layer_idx + 1], axis=0, keepdims=True)
          / jnp.sum(lbe, axis=0, keepdims=True))

    t_loc = lax.broadcasted_iota(jnp.int32, (lt, 1), 0)
    pos = (pid * lt + t_loc + 1 + n_valid).astype(jnp.float32)
    inv_cnt = [1.0 / jnp.minimum(pos, float(w)) for w in POOL_WINDOWS]

    ti = lax.broadcasted_iota(jnp.int32, (lt, lt), 0)
    si = lax.broadcasted_iota(jnp.int32, (lt, lt), 1)
    pair_xor = jnp.where(ti > si, ti ^ si, 0)
    diag = ti == si
    in_base_block = (pair_xor < BASE_BLOCK) & (ti >= si)
    row = lax.broadcasted_iota(jnp.int32, (lt, HEAD_DIM), 0)
    g_hg = g_hg_ref[...]
    pool_scale = pool_scale_ref[...]

    def seq_rows(b):
        return pl.ds(b * lt if isinstance(b, int) else pl.multiple_of(b * lt, lt), lt)

    def half_block_scan(log2f):
        t = log2f.reshape(lt // SUBLANES, SUBLANES, HEAD_DIM)
        sub = lax.broadcasted_iota(jnp.int32, t.shape, 1)
        s = 1
        while s < SUBLANES:
            t = t + jnp.where(sub >= s, pltpu.roll(t, s, 1), 0.0)
            s *= 2
        tot = jnp.broadcast_to(t[:, SUBLANES - 1:SUBLANES, :], t.shape)
        return t.reshape(lt, HEAD_DIM), tot.reshape(lt, HEAD_DIM)

    def pair_products(q, k, log2f, midpoint):
        if midpoint:
            prods = []
            c, tot = half_block_scan(log2f)
            n = BASE_BLOCK // 2
        else:
            prods = [("diag", _dot(_bf(q), _bf(k), _NT))]
            c = log2f
            tot = log2f
            n = 1
        while n < lt:
            lower = (row & n) != 0
            if midpoint and 2 * n == BASE_BLOCK:
                a = jnp.where(lower, c, c - tot)
                prods.append(("base", _dot(_bf(q * jnp.exp2(a)), _bf(k * jnp.exp2(-a)), _NT)))
            elif not midpoint or n >= BASE_BLOCK:
                prods.append((n, _dot(_bf(q * jnp.exp2(c)), _bf(k * jnp.exp2(tot - c)), _NT)))
            dn = pltpu.roll(tot, n, 0)
            up = pltpu.roll(tot, lt - n, 0)
            c = c + jnp.where(lower, dn, 0.0)
            tot = tot + jnp.where(lower, dn, up)
            n *= 2
        return prods, c, tot

    def pair_scores(prods):
        attn = None
        for which, p in prods:
            if which == "diag":
                attn = jnp.where(diag, p, 0.0)
            elif which == "base":
                attn = jnp.where(in_base_block, p, 0.0)
            else:
                attn = jnp.where((pair_xor >= which) & (pair_xor < 2 * which), p, attn)
        return attn

    def pool_mixer(b):
        rows = seq_rows(b)
        for g, w in enumerate(POOL_WINDOWS):
            cols = slice(g * POOL_GROUP, (g + 1) * POOL_GROUP)
            e = ext_ref[b, :, cols]
            win = e
            span = 1
            while span < w:
                win = win + pltpu.roll(win, span, 0)
                span *= 2
            cur = e[HIST_ROWS:]
            d = win[HIST_ROWS:] * inv_cnt[g] - cur
            yp = _dot(_bf(d), w_pool_ref[g]) * pool_scale[:, cols]
            ymix_ref[rows, cols] = _bf(yp)
        ext_ref[b, 0:HIST_ROWS, :] = ext_ref[b, lt:lt + HIST_ROWS, :]

    def head_products(b, hd, midpoint):
        rows = seq_rows(b)
        c0 = hd * HEAD_DIM
        zq = z_ref[rows, c0:c0 + HEAD_DIM]
        zf = z_ref[rows, D_HG + c0:D_HG + c0 + HEAD_DIM]
        v = _bf(z_ref[rows, 2 * D_HG + c0:2 * D_HG + c0 + HEAD_DIM])
        lbh = lb[:, c0:c0 + HEAD_DIM]

        q = _silu(zq)
        fgate = lbh + (1.0 - lbh) * (0.5 + 0.5 * jnp.tanh(0.5 * zf))
        k = 1.0 - fgate
        log2f = jnp.log(fgate) * LOG2_E

        prods, c, tot = pair_products(q, k, log2f, midpoint)

        st = s_out_ref[b, hd]
        o_state = _dot(_bf(q * jnp.exp2(c)), _bf(st), _NT)
        k_dec = _bf(k * jnp.exp2(tot - c))
        s_out_ref[b, hd] = st * jnp.exp2(tot[0:1, :]) + _dot(v, k_dec, _TN)
        return prods, v, o_state

    def head_output(b, hd, attn, v, o_state):
        rows = seq_rows(b)
        c0 = hd * HEAD_DIM
        zg = z_ref[rows, 3 * D_HG + c0:3 * D_HG + c0 + HEAD_DIM]
        o = _rms(_dot(_bf(attn), v) + o_state, g_hg) * _silu(zg)
        ymix_ref[rows, D_POOL + c0:D_POOL + c0 + HEAD_DIM] = _bf(o)

    def hgrn_heads(units, midpoint):
        parts = [head_products(b, hd, midpoint) for b, hd in units]
        for (b, hd), (prods, v, o_state) in zip(units, parts):
            head_output(b, hd, pair_scores(prods), v, o_state)

    def in_proj(s0, s1):
        xg, scale = _rms_split(x_ref[s0:s1].reshape((s1 - s0) * lt, D_MODEL), g_pre_mix_ref[...])
        h = _bf(xg)
        ext_ref[s0:s1, HIST_ROWS:HIST_ROWS + lt, :] = (
            _dot(h, w_in_ref[:, 0:D_POOL]) * scale).reshape(s1 - s0, lt, D_POOL)
        z_ref[s0 * lt:s1 * lt, :] = _dot(h, w_in_ref[:, D_POOL:D_IN]) * scale
        for b in range(s0, s1):
            pool_mixer(b)

    def channel_mix(bounds, between=()):
        groups = range(len(bounds))
        rows = [slice(s0 * lt, s1 * lt) for s0, s1 in bounds]
        yproj = [_dot(ymix_ref[rows[gi], :], w_out_ref[...]) for gi in groups]
        x1 = [x_ref[bounds[gi][0]:bounds[gi][1]].reshape(-1, D_MODEL)
              + _rms(yproj[gi], g_post_mix_ref[...]) for gi in groups]
        split = [_rms_split(x1[gi], g_pre_ffn_ref[...]) for gi in groups]
        h2 = [_bf(split[gi][0]) for gi in groups]
        f = [None for _ in groups]
        for ci, c0 in enumerate(range(0, w_gate_ref.shape[1], FFN_CHUNK)):
            cols = slice(c0, min(c0 + FFN_CHUNK, w_gate_ref.shape[1]))
            if ci < len(between):
                between[ci]()
            for gi in groups:
                gate = _dot(h2[gi], w_gate_ref[:, cols]) * split[gi][1]
                upp = _dot(h2[gi], w_up_ref[:, cols]) * split[gi][1]
                act = _bf(_silu(gate) * upp)
                part = _dot(act, w_down_ref[cols, :])
                f[gi] = part if f[gi] is None else f[gi] + part
        for gi, (s0, s1) in enumerate(bounds):
            y_ref[s0:s1] = (x1[gi] + _rms(f[gi], g_post_ffn_ref[...])).reshape(s1 - s0, lt, D_MODEL)

    def step_midpoint():
        half = nb // 2
        early = [(b, hd) for b in range(0, half) for hd in range(N_HEADS)]
        late = [(b, hd) for b in range(half, nb) for hd in range(N_HEADS)]
        n_slots = -(-w_gate_ref.shape[1] // FFN_CHUNK)
        per_slot = -(-len(late) // n_slots)
        in_proj(0, half)
        in_proj(half, nb)
        hgrn_heads(early, True)
        channel_mix([(0, half)],
                    between=[functools.partial(hgrn_heads, late[i:i + per_slot], True)
                             for i in range(0, len(late), per_slot)])
        channel_mix([(half, nb)])

    def step_robust():
        in_proj(0, nb)
        lax.fori_loop(
            0, nb // SEQ_GROUP,
            lambda i, carry: hgrn_heads([(i * SEQ_GROUP + j, hd) for j in range(SEQ_GROUP)
                                         for hd in range(N_HEADS)], False), None)
        channel_mix([(0, nb // 2), (nb // 2, nb)])

    lax.cond(jnp.min(lb) >= MIDPOINT_MIN_LB, step_midpoint, step_robust)

    @pl.when(pid == last)
    def _carry_out():
        pool_out_ref[...] = ext_ref[:, 1:HIST_ROWS, :]
        for i in range(nb):
            for hd in range(N_HEADS):
                s_out_ref[i, hd] = s_out_ref[i, hd].T


def _const_spec(shape):
    zeros = (0,) * len(shape)
    return pl.BlockSpec(shape, lambda i: zeros, pipeline_mode=pl.Buffered(1))


def _layer_call(x, hist, s0, lb_logits, weights, *, lt, n_valid, layer_idx):
    nb, seq, _ = x.shape
    assert seq % lt == 0 and lt % BASE_BLOCK == 0 and lt >= HIST_ROWS and (lt & (lt - 1)) == 0
    assert nb % SEQ_GROUP == 0 and nb % 2 == 0
    m = nb * lt
    (g_pre_mix, w_in, w_pool, pool_scale, g_hg, w_out, g_post_mix, g_pre_ffn,
     w_gate, w_up, w_down, g_post_ffn) = weights
    consts = (hist, s0, lb_logits, g_pre_mix, w_in, w_pool, pool_scale, g_hg, w_out,
              g_post_mix, g_pre_ffn, w_gate, w_up, w_down, g_post_ffn)
    x_spec = pl.BlockSpec((nb, lt, D_MODEL), lambda i: (0, i, 0))
    kern = functools.partial(_layer_kernel, nb=nb, lt=lt, n_valid=n_valid, layer_idx=layer_idx)
    return pl.pallas_call(
        kern,
        grid=(seq // lt,),
        in_specs=[x_spec] + [_const_spec(c.shape) for c in consts],
        out_specs=[x_spec, _const_spec((nb, POOL_HIST, D_POOL)),
                   _const_spec((nb, N_HEADS, HEAD_DIM, HEAD_DIM))],
        out_shape=[jax.ShapeDtypeStruct(x.shape, x.dtype),
                   jax.ShapeDtypeStruct((nb, POOL_HIST, D_POOL), x.dtype),
                   jax.ShapeDtypeStruct((nb, N_HEADS, HEAD_DIM, HEAD_DIM), jnp.float32)],
        scratch_shapes=[pltpu.VMEM((m, 4 * D_HG), jnp.float32),
                        pltpu.VMEM((nb, HIST_ROWS + lt, D_POOL), jnp.float32),
                        pltpu.VMEM((m, D_MODEL), jnp.bfloat16)],
        compiler_params=pltpu.CompilerParams(dimension_semantics=("arbitrary",),
                                             vmem_limit_bytes=VMEM_LIMIT_BYTES),
        name=f"layer_lt{lt}",
    )(x, *consts)


def kernel(x_prompt, x_sample, cache_pool, state_hgrn, g_pre_mix, w_in, w_pool, pool_scale,
           lb_logits, g_hg_norm, w_out, g_post_mix, g_pre_ffn, w_gate, w_up, w_down, g_post_ffn):
    depth = w_in.shape[0]
    xp, xs = x_prompt, x_sample
    pools_p, hgrn_p, pools_s, hgrn_s = [], [], [], []
    for l in range(depth):
        row = lambda a: a[l][None, :]
        weights = (row(g_pre_mix), _bf(w_in[l]), _bf(w_pool[l]), row(pool_scale), row(g_hg_norm),
                   _bf(w_out[l]), row(g_post_mix), row(g_pre_ffn), _bf(w_gate[l]), _bf(w_up[l]),
                   _bf(w_down[l]), row(g_post_ffn))
        hist0 = jnp.zeros((xp.shape[0], POOL_HIST, D_POOL), xp.dtype)
        s00 = jnp.zeros((xp.shape[0], N_HEADS, HEAD_DIM, HEAD_DIM), jnp.float32)
        xp, pool_p, s_p = _layer_call(xp, hist0, s00, lb_logits, weights,
                                      lt=PROMPT_CHUNK, n_valid=0, layer_idx=l)
        xs, pool_s, s_s = _layer_call(xs, cache_pool[l], state_hgrn[l], lb_logits, weights,
                                      lt=xs.shape[1], n_valid=POOL_HIST, layer_idx=l)
        pools_p.append(pool_p)
        hgrn_p.append(s_p)
        pools_s.append(pool_s)
        hgrn_s.append(s_s)
    return (xp, xs, jnp.stack(pools_p), jnp.stack(hgrn_p), jnp.stack(pools_s), jnp.stack(hgrn_s))
```

```python
import functools
import math

import jax
import jax.numpy as jnp
from jax import lax
from jax.experimental import pallas as pl
from jax.experimental.pallas import tpu as pltpu

D_MODEL = 1024
D_POOL = 512
D_HG = 512
POOL_WINDOWS = (2, 4, 8, 16)
POOL_GROUP = D_POOL // len(POOL_WINDOWS)
POOL_HIST = max(POOL_WINDOWS) - 1
HIST_ROWS = POOL_HIST + 1
HEAD_DIM = 128
N_HEADS = D_HG // HEAD_DIM
D_IN = D_POOL + 4 * D_HG
EPS = 1e-6
LOG2_E = math.log2(math.e)
PROMPT_CHUNK = 64
FFN_CHUNK = 1024
SEQ_GROUP = 4

BASE_BLOCK = 16
SUBLANES = 8
assert BASE_BLOCK == 2 * SUBLANES
MAX_FACTOR_LOG = 40.0
MIDPOINT_MIN_LB = math.exp(-MAX_FACTOR_LOG / (BASE_BLOCK // 2 - 1))

V7X_VMEM_BYTES = 64 * 1024 * 1024
VMEM_LIMIT_BYTES = V7X_VMEM_BYTES - 8 * 1024 * 1024

_NT = (((1,), (1,)), ((), ()))
_TN = (((0,), (0,)), ((), ()))


def _rms(x, g):
    return x * lax.rsqrt(jnp.mean(x * x, axis=-1, keepdims=True) + EPS) * g


def _rms_split(x, g):
    return x * g, lax.rsqrt(jnp.mean(x * x, axis=-1, keepdims=True) + EPS)


def _silu(x):
    u = 0.5 * x
    return u + u * jnp.tanh(u)


def _bf(x):
    return x.astype(jnp.bfloat16)


def _dot(a, b, dims=None):
    if dims is None:
        return jnp.dot(a, b, preferred_element_type=jnp.float32)
    return lax.dot_general(a, b, dims, preferred_element_type=jnp.float32)


def _layer_kernel(x_ref, hist_ref, s0_ref, lbl_ref, g_pre_mix_ref, w_in_ref, w_pool_ref,
                  pool_scale_ref, g_hg_ref, w_out_ref, g_post_mix_ref, g_pre_ffn_ref,
                  w_gate_ref, w_up_ref, w_down_ref, g_post_ffn_ref,
                  y_ref, pool_out_ref, s_out_ref,
                  z_ref, ext_ref, ymix_ref,
                  *, nb, lt, n_valid, layer_idx, threaded):
    pid = pl.program_id(0)
    last = pl.num_programs(0) - 1
    m = nb * lt

    @pl.when(pid == 0)
    def _init():
        ext_ref[:, 0:1, :] = jnp.zeros((nb, 1, D_POOL), jnp.float32)
        ext_ref[:, 1:HIST_ROWS, :] = hist_ref[...]
        for i in range(nb):
            for h in range(N_HEADS):
                s_out_ref[i, h] = s0_ref[i, h].T

    lbl = lbl_ref[...]
    lbe = jnp.exp(lbl - jnp.max(lbl, axis=0, keepdims=True))
    lb = (jnp.sum(lbe[:layer_idx + 1], axis=0, keepdims=True)
          / jnp.sum(lbe, axis=0, keepdims=True))

    t_loc = lax.broadcasted_iota(jnp.int32, (lt, 1), 0)
    pos = (pid * lt + t_loc + 1 + n_valid).astype(jnp.float32)
    inv_cnt = [1.0 / jnp.minimum(pos, float(w)) for w in POOL_WINDOWS]

    ti = lax.broadcasted_iota(jnp.int32, (lt, lt), 0)
    si = lax.broadcasted_iota(jnp.int32, (lt, lt), 1)
    pair_xor = jnp.where(ti > si, ti ^ si, 0)
    diag = ti == si
    in_base_block = (pair_xor < BASE_BLOCK) & (ti >= si)
    row = lax.broadcasted_iota(jnp.int32, (lt, HEAD_DIM), 0)
    g_hg = g_hg_ref[...]
    pool_scale = pool_scale_ref[...]

    def seq_rows(b):
        return pl.ds(b * lt if isinstance(b, int) else pl.multiple_of(b * lt, lt), lt)

    def half_block_scan(log2f):
        t = log2f.reshape(lt // SUBLANES, SUBLANES, HEAD_DIM)
        sub = lax.broadcasted_iota(jnp.int32, t.shape, 1)
        s = 1
        while s < SUBLANES:
            t = t + jnp.where(sub >= s, pltpu.roll(t, s, 1), 0.0)
            s *= 2
        tot = jnp.broadcast_to(t[:, SUBLANES - 1:SUBLANES, :], t.shape)
        return t.reshape(lt, HEAD_DIM), tot.reshape(lt, HEAD_DIM)

    def pair_products(q, k, log2f, midpoint):
        if midpoint:
            prods = []
            c, tot = half_block_scan(log2f)
            n = BASE_BLOCK // 2
        else:
            prods = [("diag", _dot(_bf(q), _bf(k), _NT))]
            c = log2f
            tot = log2f
            n = 1
        while n < lt:
            lower = (row & n) != 0
            if midpoint and 2 * n == BASE_BLOCK:
                a = jnp.where(lower, c, c - tot)
                prods.append(("base", _dot(_bf(q * jnp.exp2(a)), _bf(k * jnp.exp2(-a)), _NT)))
            elif not midpoint or n >= BASE_BLOCK:
                prods.append((n, _dot(_bf(q * jnp.exp2(c)), _bf(k * jnp.exp2(tot - c)), _NT)))
            dn = pltpu.roll(tot, n, 0)
            up = pltpu.roll(tot, lt - n, 0)
            c = c + jnp.where(lower, dn, 0.0)
            tot = tot + jnp.where(lower, dn, up)
            n *= 2
        return prods, c, tot

    def pair_scores(prods):
        attn = None
        for which, p in prods:
            if which == "diag":
                attn = jnp.where(diag, p, 0.0)
            elif which == "base":
                attn = jnp.where(in_base_block, p, 0.0)
            else:
                attn = jnp.where((pair_xor >= which) & (pair_xor < 2 * which), p, attn)
        return attn

    def pool_mixer(b):
        rows = seq_rows(b)
        for g, w in enumerate(POOL_WINDOWS):
            cols = slice(g * POOL_GROUP, (g + 1) * POOL_GROUP)
            e = ext_ref[b, :, cols]
            win = e
            span = 1
            while span < w:
                win = win + pltpu.roll(win, span, 0)
                span *= 2
            cur = e[HIST_ROWS:]
            d = win[HIST_ROWS:] * inv_cnt[g] - cur
            yp = _dot(_bf(d), w_pool_ref[g]) * pool_scale[:, cols]
            ymix_ref[rows, cols] = _bf(yp)
        ext_ref[b, 0:HIST_ROWS, :] = ext_ref[b, lt:lt + HIST_ROWS, :]

    def head_products(b, hd, midpoint):
        rows = seq_rows(b)
        c0 = hd * HEAD_DIM
        zq = z_ref[rows, c0:c0 + HEAD_DIM]
        zf = z_ref[rows, D_HG + c0:D_HG + c0 + HEAD_DIM]
        v = _bf(z_ref[rows, 2 * D_HG + c0:2 * D_HG + c0 + HEAD_DIM])
        lbh = lb[:, c0:c0 + HEAD_DIM]

        q = _silu(zq)
        fgate = lbh + (1.0 - lbh) * (0.5 + 0.5 * jnp.tanh(0.5 * zf))
        k = 1.0 - fgate
        log2f = jnp.log(fgate) * LOG2_E

        prods, c, tot = pair_products(q, k, log2f, midpoint)

        st = s_out_ref[b, hd]
        o_state = _dot(_bf(q * jnp.exp2(c)), _bf(st), _NT)
        k_dec = _bf(k * jnp.exp2(tot - c))
        s_out_ref[b, hd] = st * jnp.exp2(tot[0:1, :]) + _dot(v, k_dec, _TN)
        return prods, v, o_state

    def head_output(b, hd, attn, v, o_state):
        rows = seq_rows(b)
        c0 = hd * HEAD_DIM
        zg = z_ref[rows, 3 * D_HG + c0:3 * D_HG + c0 + HEAD_DIM]
        o = _rms(_dot(_bf(attn), v) + o_state, g_hg) * _silu(zg)
        ymix_ref[rows, D_POOL + c0:D_POOL + c0 + HEAD_DIM] = _bf(o)

    def hgrn_heads(units, midpoint):
        parts = [head_products(b, hd, midpoint) for b, hd in units]
        for (b, hd), (prods, v, o_state) in zip(units, parts):
            head_output(b, hd, pair_scores(prods), v, o_state)

    def in_proj(s0, s1):
        xg, scale = _rms_split(x_ref[s0:s1].reshape((s1 - s0) * lt, D_MODEL), g_pre_mix_ref[...])
        h = _bf(xg)
        ext_ref[s0:s1, HIST_ROWS:HIST_ROWS + lt, :] = (
            _dot(h, w_in_ref[:, 0:D_POOL]) * scale).reshape(s1 - s0, lt, D_POOL)
        z_ref[s0 * lt:s1 * lt, :] = _dot(h, w_in_ref[:, D_POOL:D_IN]) * scale
        for b in range(s0, s1):
            pool_mixer(b)

    def channel_mix(bounds, between=()):
        groups = range(len(bounds))
        rows = [slice(s0 * lt, s1 * lt) for s0, s1 in bounds]
        yproj = [_dot(ymix_ref[rows[gi], :], w_out_ref[...]) for gi in groups]
        x1 = [x_ref[bounds[gi][0]:bounds[gi][1]].reshape(-1, D_MODEL)
              + _rms(yproj[gi], g_post_mix_ref[...]) for gi in groups]
        split = [_rms_split(x1[gi], g_pre_ffn_ref[...]) for gi in groups]
        h2 = [_bf(split[gi][0]) for gi in groups]
        f = [None for _ in groups]
        for ci, c0 in enumerate(range(0, w_gate_ref.shape[1], FFN_CHUNK)):
            cols = slice(c0, min(c0 + FFN_CHUNK, w_gate_ref.shape[1]))
            if ci < len(between):
                between[ci]()
            for gi in groups:
                gate = _dot(h2[gi], w_gate_ref[:, cols]) * split[gi][1]
                upp = _dot(h2[gi], w_up_ref[:, cols]) * split[gi][1]
                act = _bf(_silu(gate) * upp)
                part = _dot(act, w_down_ref[cols, :])
                f[gi] = part if f[gi] is None else f[gi] + part
        for gi, (s0, s1) in enumerate(bounds):
            y_ref[s0:s1] = (x1[gi] + _rms(f[gi], g_post_ffn_ref[...])).reshape(s1 - s0, lt, D_MODEL)

    def step_midpoint():
        half = nb // 2
        early = [(b, hd) for b in range(0, half) for hd in range(N_HEADS)]
        late = [(b, hd) for b in range(half, nb) for hd in range(N_HEADS)]
        n_slots = -(-w_gate_ref.shape[1] // FFN_CHUNK)
        per_slot = -(-len(late) // n_slots)
        in_proj(0, half)
        in_proj(half, nb)
        hgrn_heads(early, True)
        channel_mix([(0, half)],
                    between=[functools.partial(hgrn_heads, late[i:i + per_slot], True)
                             for i in range(0, len(late), per_slot)])
        channel_mix([(half, nb)])

    def step_looped(midpoint):
        in_proj(0, nb)
        lax.fori_loop(
            0, nb // SEQ_GROUP,
            lambda i, carry: hgrn_heads([(i * SEQ_GROUP + j, hd) for j in range(SEQ_GROUP)
                                         for hd in range(N_HEADS)], midpoint), None)
        channel_mix([(0, nb // 2), (nb // 2, nb)])

    lax.cond(jnp.min(lb) >= MIDPOINT_MIN_LB,
             step_midpoint if threaded else functools.partial(step_looped, True),
             functools.partial(step_looped, False))

    @pl.when(pid == last)
    def _carry_out():
        pool_out_ref[...] = ext_ref[:, 1:HIST_ROWS, :]
        for i in range(nb):
            for hd in range(N_HEADS):
                s_out_ref[i, hd] = s_out_ref[i, hd].T


def _const_spec(shape):
    zeros = (0,) * len(shape)
    return pl.BlockSpec(shape, lambda i: zeros, pipeline_mode=pl.Buffered(1))


def _layer_call(x, hist, s0, lb_logits, weights, *, lt, n_valid, layer_idx):
    nb, seq, _ = x.shape
    assert seq % lt == 0 and lt % BASE_BLOCK == 0 and lt >= HIST_ROWS and (lt & (lt - 1)) == 0
    assert nb % SEQ_GROUP == 0 and nb % 2 == 0
    m = nb * lt
    (g_pre_mix, w_in, w_pool, pool_scale, g_hg, w_out, g_post_mix, g_pre_ffn,
     w_gate, w_up, w_down, g_post_ffn) = weights
    consts = (hist, s0, lb_logits, g_pre_mix, w_in, w_pool, pool_scale, g_hg, w_out,
              g_post_mix, g_pre_ffn, w_gate, w_up, w_down, g_post_ffn)
    x_spec = pl.BlockSpec((nb, lt, D_MODEL), lambda i: (0, i, 0))
    kern = functools.partial(_layer_kernel, nb=nb, lt=lt, n_valid=n_valid, layer_idx=layer_idx,
                             threaded=seq // lt > 1)
    return pl.pallas_call(
        kern,
        grid=(seq // lt,),
        in_specs=[x_spec] + [_const_spec(c.shape) for c in consts],
        out_specs=[x_spec, _const_spec((nb, POOL_HIST, D_POOL)),
                   _const_spec((nb, N_HEADS, HEAD_DIM, HEAD_DIM))],
        out_shape=[jax.ShapeDtypeStruct(x.shape, x.dtype),
                   jax.ShapeDtypeStruct((nb, POOL_HIST, D_POOL), x.dtype),
                   jax.ShapeDtypeStruct((nb, N_HEADS, HEAD_DIM, HEAD_DIM), jnp.float32)],
        scratch_shapes=[pltpu.VMEM((m, 4 * D_HG), jnp.float32),
                        pltpu.VMEM((nb, HIST_ROWS + lt, D_POOL), jnp.float32),
                        pltpu.VMEM((m, D_MODEL), jnp.bfloat16)],
        compiler_params=pltpu.CompilerParams(dimension_semantics=("arbitrary",),
                                             vmem_limit_bytes=VMEM_LIMIT_BYTES),
        name=f"layer_lt{lt}",
    )(x, *consts)


def kernel(x_prompt, x_sample, cache_pool, state_hgrn, g_pre_mix, w_in, w_pool, pool_scale,
           lb_logits, g_hg_norm, w_out, g_post_mix, g_pre_ffn, w_gate, w_up, w_down, g_post_ffn):
    depth = w_in.shape[0]
    xp, xs = x_prompt, x_sample
    pools_p, hgrn_p, pools_s, hgrn_s = [], [], [], []
    for l in range(depth):
        row = lambda a: a[l][None, :]
        weights = (row(g_pre_mix), _bf(w_in[l]), _bf(w_pool[l]), row(pool_scale), row(g_hg_norm),
                   _bf(w_out[l]), row(g_post_mix), row(g_pre_ffn), _bf(w_gate[l]), _bf(w_up[l]),
                   _bf(w_down[l]), row(g_post_ffn))
        hist0 = jnp.zeros((xp.shape[0], POOL_HIST, D_POOL), xp.dtype)
        s00 = jnp.zeros((xp.shape[0], N_HEADS, HEAD_DIM, HEAD_DIM), jnp.float32)
        xp, pool_p, s_p = _layer_call(xp, hist0, s00, lb_logits, weights,
                                      lt=PROMPT_CHUNK, n_valid=0, layer_idx=l)
        xs, pool_s, s_s = _layer_call(xs, cache_pool[l], state_hgrn[l], lb_logits, weights,
                                      lt=xs.shape[1], n_valid=POOL_HIST, layer_idx=l)
        pools_p.append(pool_p)
        hgrn_p.append(s_p)
        pools_s.append(pool_s)
        hgrn_s.append(s_s)
    return (xp, xs, jnp.stack(pools_p), jnp.stack(hgrn_p), jnp.stack(pools_s), jnp.stack(hgrn_s))
```

```python
import functools
import math

import jax
import jax.numpy as jnp
from jax import lax
from jax.experimental import pallas as pl
from jax.experimental.pallas import tpu as pltpu

D_MODEL = 1024
D_POOL = 512
D_HG = 512
POOL_WINDOWS = (2, 4, 8, 16)
POOL_GROUP = D_POOL // len(POOL_WINDOWS)
POOL_HIST = max(POOL_WINDOWS) - 1
HIST_ROWS = POOL_HIST + 1
HEAD_DIM = 128
N_HEADS = D_HG // HEAD_DIM
D_IN = D_POOL + 4 * D_HG
EPS = 1e-6
LOG2_E = math.log2(math.e)
PROMPT_CHUNK = 64
FFN_CHUNK = 1024
SEQ_GROUP = 8

BASE_BLOCK = 16
SUBLANES = 8
assert BASE_BLOCK == 2 * SUBLANES
MAX_FACTOR_LOG = 40.0
MIDPOINT_MIN_LB = math.exp(-MAX_FACTOR_LOG / (BASE_BLOCK // 2 - 1))

V7X_VMEM_BYTES = 64 * 1024 * 1024
VMEM_LIMIT_BYTES = V7X_VMEM_BYTES - 8 * 1024 * 1024

_NT = (((1,), (1,)), ((), ()))
_TN = (((0,), (0,)), ((), ()))


def _rms(x, g):
    return x * lax.rsqrt(jnp.mean(x * x, axis=-1, keepdims=True) + EPS) * g


def _rms_split(x, g):
    return x * g, lax.rsqrt(jnp.mean(x * x, axis=-1, keepdims=True) + EPS)


def _silu(x):
    u = 0.5 * x
    return u + u * jnp.tanh(u)


def _bf(x):
    return x.astype(jnp.bfloat16)


def _dot(a, b, dims=None):
    if dims is None:
        return jnp.dot(a, b, preferred_element_type=jnp.float32)
    return lax.dot_general(a, b, dims, preferred_element_type=jnp.float32)


def _layer_kernel(x_ref, hist_ref, s0_ref, lbl_ref, g_pre_mix_ref, w_in_ref, w_pool_ref,
                  pool_scale_ref, g_hg_ref, w_out_ref, g_post_mix_ref, g_pre_ffn_ref,
                  w_gate_ref, w_up_ref, w_down_ref, g_post_ffn_ref,
                  y_ref, pool_out_ref, s_out_ref,
                  z_ref, ext_ref, ymix_ref,
                  *, nb, lt, n_valid, layer_idx):
    pid = pl.program_id(0)
    last = pl.num_programs(0) - 1
    m = nb * lt

    @pl.when(pid == 0)
    def _init():
        ext_ref[:, 0:1, :] = jnp.zeros((nb, 1, D_POOL), jnp.float32)
        ext_ref[:, 1:HIST_ROWS, :] = hist_ref[...]
        for i in range(nb):
            for h in range(N_HEADS):
                s_out_ref[i, h] = s0_ref[i, h].T

    lbl = lbl_ref[...]
    lbe = jnp.exp(lbl - jnp.max(lbl, axis=0, keepdims=True))
    lb = (jnp.sum(lbe[:layer_idx + 1], axis=0, keepdims=True)
          / jnp.sum(lbe, axis=0, keepdims=True))

    t_loc = lax.broadcasted_iota(jnp.int32, (lt, 1), 0)
    pos = (pid * lt + t_loc + 1 + n_valid).astype(jnp.float32)
    inv_cnt = [1.0 / jnp.minimum(pos, float(w)) for w in POOL_WINDOWS]

    ti = lax.broadcasted_iota(jnp.int32, (lt, lt), 0)
    si = lax.broadcasted_iota(jnp.int32, (lt, lt), 1)
    pair_xor = jnp.where(ti > si, ti ^ si, 0)
    diag = ti == si
    in_base_block = (pair_xor < BASE_BLOCK) & (ti >= si)
    row = lax.broadcasted_iota(jnp.int32, (lt, HEAD_DIM), 0)
    g_hg = g_hg_ref[...]
    pool_scale = pool_scale_ref[...]

    def seq_rows(b):
        return pl.ds(b * lt if isinstance(b, int) else pl.multiple_of(b * lt, lt), lt)

    def half_block_scan(log2f):
        t = log2f.reshape(lt // SUBLANES, SUBLANES, HEAD_DIM)
        sub = lax.broadcasted_iota(jnp.int32, t.shape, 1)
        s = 1
        while s < SUBLANES:
            t = t + jnp.where(sub >= s, pltpu.roll(t, s, 1), 0.0)
            s *= 2
        tot = jnp.broadcast_to(t[:, SUBLANES - 1:SUBLANES, :], t.shape)
        return t.reshape(lt, HEAD_DIM), tot.reshape(lt, HEAD_DIM)

    def pair_products(q, k, log2f, midpoint):
        if midpoint:
            prods = []
            c, tot = half_block_scan(log2f)
            n = BASE_BLOCK // 2
        else:
            prods = [("diag", _dot(_bf(q), _bf(k), _NT))]
            c = log2f
            tot = log2f
            n = 1
        while n < lt:
            lower = (row & n) != 0
            if midpoint and 2 * n == BASE_BLOCK:
                a = jnp.where(lower, c, c - tot)
                prods.append(("base", _dot(_bf(q * jnp.exp2(a)), _bf(k * jnp.exp2(-a)), _NT)))
            elif not midpoint or n >= BASE_BLOCK:
                prods.append((n, _dot(_bf(q * jnp.exp2(c)), _bf(k * jnp.exp2(tot - c)), _NT)))
            dn = pltpu.roll(tot, n, 0)
            up = pltpu.roll(tot, lt - n, 0)
            c = c + jnp.where(lower, dn, 0.0)
            tot = tot + jnp.where(lower, dn, up)
            n *= 2
        return prods, c, tot

    def pair_scores(prods):
        attn = None
        for which, p in prods:
            if which == "diag":
                attn = jnp.where(diag, p, 0.0)
            elif which == "base":
                attn = jnp.where(in_base_block, p, 0.0)
            else:
                attn = jnp.where((pair_xor >= which) & (pair_xor < 2 * which), p, attn)
        return attn

    def pool_mixer(b):
        rows = seq_rows(b)
        for g, w in enumerate(POOL_WINDOWS):
            cols = slice(g * POOL_GROUP, (g + 1) * POOL_GROUP)
            e = ext_ref[b, :, cols]
            win = e
            span = 1
            while span < w:
                win = win + pltpu.roll(win, span, 0)
                span *= 2
            cur = e[HIST_ROWS:]
            d = win[HIST_ROWS:] * inv_cnt[g] - cur
            yp = _dot(_bf(d), w_pool_ref[g]) * pool_scale[:, cols]
            ymix_ref[rows, cols] = _bf(yp)
        ext_ref[b, 0:HIST_ROWS, :] = ext_ref[b, lt:lt + HIST_ROWS, :]

    def head_products(b, hd, midpoint):
        rows = seq_rows(b)
        c0 = hd * HEAD_DIM
        zq = z_ref[rows, c0:c0 + HEAD_DIM]
        zf = z_ref[rows, D_HG + c0:D_HG + c0 + HEAD_DIM]
        v = _bf(z_ref[rows, 2 * D_HG + c0:2 * D_HG + c0 + HEAD_DIM])
        lbh = lb[:, c0:c0 + HEAD_DIM]

        q = _silu(zq)
        fgate = lbh + (1.0 - lbh) * (0.5 + 0.5 * jnp.tanh(0.5 * zf))
        k = 1.0 - fgate
        log2f = jnp.log(fgate) * LOG2_E

        prods, c, tot = pair_products(q, k, log2f, midpoint)

        st = s_out_ref[b, hd]
        o_state = _dot(_bf(q * jnp.exp2(c)), _bf(st), _NT)
        k_dec = _bf(k * jnp.exp2(tot - c))
        s_out_ref[b, hd] = st * jnp.exp2(tot[0:1, :]) + _dot(v, k_dec, _TN)
        return prods, v, o_state

    def head_output(b, hd, attn, v, o_state):
        rows = seq_rows(b)
        c0 = hd * HEAD_DIM
        zg = z_ref[rows, 3 * D_HG + c0:3 * D_HG + c0 + HEAD_DIM]
        o = _rms(_dot(_bf(attn), v) + o_state, g_hg) * _silu(zg)
        ymix_ref[rows, D_POOL + c0:D_POOL + c0 + HEAD_DIM] = _bf(o)

    def hgrn_heads(units, midpoint):
        parts = [head_products(b, hd, midpoint) for b, hd in units]
        for (b, hd), (prods, v, o_state) in zip(units, parts):
            head_output(b, hd, pair_scores(prods), v, o_state)

    def in_proj(s0, s1):
        xg, scale = _rms_split(x_ref[s0:s1].reshape((s1 - s0) * lt, D_MODEL), g_pre_mix_ref[...])
        h = _bf(xg)
        ext_ref[s0:s1, HIST_ROWS:HIST_ROWS + lt, :] = (
            _dot(h, w_in_ref[:, 0:D_POOL]) * scale).reshape(s1 - s0, lt, D_POOL)
        z_ref[s0 * lt:s1 * lt, :] = _dot(h, w_in_ref[:, D_POOL:D_IN]) * scale
        for b in range(s0, s1):
            pool_mixer(b)

    def channel_mix(bounds):
        groups = range(len(bounds))
        rows = [slice(s0 * lt, s1 * lt) for s0, s1 in bounds]
        yproj = [_dot(ymix_ref[rows[gi], :], w_out_ref[...]) for gi in groups]
        x1 = [x_ref[bounds[gi][0]:bounds[gi][1]].reshape(-1, D_MODEL)
              + _rms(yproj[gi], g_post_mix_ref[...]) for gi in groups]
        split = [_rms_split(x1[gi], g_pre_ffn_ref[...]) for gi in groups]
        h2 = [_bf(split[gi][0]) for gi in groups]
        f = [None for _ in groups]
        for c0 in range(0, w_gate_ref.shape[1], FFN_CHUNK):
            cols = slice(c0, min(c0 + FFN_CHUNK, w_gate_ref.shape[1]))
            for gi in groups:
                gate = _dot(h2[gi], w_gate_ref[:, cols]) * split[gi][1]
                upp = _dot(h2[gi], w_up_ref[:, cols]) * split[gi][1]
                act = _bf(_silu(gate) * upp)
                part = _dot(act, w_down_ref[cols, :])
                f[gi] = part if f[gi] is None else f[gi] + part
        for gi, (s0, s1) in enumerate(bounds):
            y_ref[s0:s1] = (x1[gi] + _rms(f[gi], g_post_ffn_ref[...])).reshape(s1 - s0, lt, D_MODEL)

    def step(midpoint):
        in_proj(0, nb)
        lax.fori_loop(
            0, nb // SEQ_GROUP,
            lambda i, carry: hgrn_heads([(i * SEQ_GROUP + j, hd) for j in range(SEQ_GROUP)
                                         for hd in range(N_HEADS)], midpoint), None)
        channel_mix([(0, nb // 2), (nb // 2, nb)])

    lax.cond(jnp.min(lb) >= MIDPOINT_MIN_LB,
             functools.partial(step, True), functools.partial(step, False))

    @pl.when(pid == last)
    def _carry_out():
        pool_out_ref[...] = ext_ref[:, 1:HIST_ROWS, :]
        for i in range(nb):
            for hd in range(N_HEADS):
                s_out_ref[i, hd] = s_out_ref[i, hd].T


def _const_spec(shape):
    zeros = (0,) * len(shape)
    return pl.BlockSpec(shape, lambda i: zeros, pipeline_mode=pl.Buffered(1))


def _layer_call(x, hist, s0, lb_logits, weights, *, lt, n_valid, layer_idx):
    nb, seq, _ = x.shape
    assert seq % lt == 0 and lt % BASE_BLOCK == 0 and lt >= HIST_ROWS and (lt & (lt - 1)) == 0
    assert nb % SEQ_GROUP == 0 and nb % 2 == 0
    m = nb * lt
    (g_pre_mix, w_in, w_pool, pool_scale, g_hg, w_out, g_post_mix, g_pre_ffn,
     w_gate, w_up, w_down, g_post_ffn) = weights
    consts = (hist, s0, lb_logits, g_pre_mix, w_in, w_pool, pool_scale, g_hg, w_out,
              g_post_mix, g_pre_ffn, w_gate, w_up, w_down, g_post_ffn)
    x_spec = pl.BlockSpec((nb, lt, D_MODEL), lambda i: (0, i, 0))
    kern = functools.partial(_layer_kernel, nb=nb, lt=lt, n_valid=n_valid, layer_idx=layer_idx)
    return pl.pallas_call(
        kern,
        grid=(seq // lt,),
        in_specs=[x_spec] + [_const_spec(c.shape) for c in consts],
        out_specs=[x_spec, _const_spec((nb, POOL_HIST, D_POOL)),
                   _const_spec((nb, N_HEADS, HEAD_DIM, HEAD_DIM))],
        out_shape=[jax.ShapeDtypeStruct(x.shape, x.dtype),
                   jax.ShapeDtypeStruct((nb, POOL_HIST, D_POOL), x.dtype),
                   jax.ShapeDtypeStruct((nb, N_HEADS, HEAD_DIM, HEAD_DIM), jnp.float32)],
        scratch_shapes=[pltpu.VMEM((m, 4 * D_HG), jnp.float32),
                        pltpu.VMEM((nb, HIST_ROWS + lt, D_POOL), jnp.float32),
                        pltpu.VMEM((m, D_MODEL), jnp.bfloat16)],
        compiler_params=pltpu.CompilerParams(dimension_semantics=("arbitrary",),
                                             vmem_limit_bytes=VMEM_LIMIT_BYTES),
        name=f"layer_lt{lt}",
    )(x, *consts)


def kernel(x_prompt, x_sample, cache_pool, state_hgrn, g_pre_mix, w_in, w_pool, pool_scale,
           lb_logits, g_hg_norm, w_out, g_post_mix, g_pre_ffn, w_gate, w_up, w_down, g_post_ffn):
    depth = w_in.shape[0]
    xp, xs = x_prompt, x_sample
    pools_p, hgrn_p, pools_s, hgrn_s = [], [], [], []
    for l in range(depth):
        row = lambda a: a[l][None, :]
        weights = (row(g_pre_mix), _bf(w_in[l]), _bf(w_pool[l]), row(pool_scale), row(g_hg_norm),
                   _bf(w_out[l]), row(g_post_mix), row(g_pre_ffn), _bf(w_gate[l]), _bf(w_up[l]),
                   _bf(w_down[l]), row(g_post_ffn))
        hist0 = jnp.zeros((xp.shape[0], POOL_HIST, D_POOL), xp.dtype)
        s00 = jnp.zeros((xp.shape[0], N_HEADS, HEAD_DIM, HEAD_DIM), jnp.float32)
        xp, pool_p, s_p = _layer_call(xp, hist0, s00, lb_logits, weights,
                                      lt=PROMPT_CHUNK, n_valid=0, layer_idx=l)
        xs, pool_s, s_s = _layer_call(xs, cache_pool[l], state_hgrn[l], lb_logits, weights,
                                      lt=xs.shape[1], n_valid=POOL_HIST, layer_idx=l)
        pools_p.append(pool_p)
        hgrn_p.append(s_p)
        pools_s.append(pool_s)
        hgrn_s.append(s_s)
    return (xp, xs, jnp.stack(pools_p), jnp.stack(hgrn_p), jnp.stack(pools_s), jnp.stack(hgrn_s))
```

```python
import functools
import math

import jax
import jax.numpy as jnp
from jax import lax
from jax.experimental import pallas as pl
from jax.experimental.pallas import tpu as pltpu

D_MODEL = 1024
D_POOL = 512
D_HG = 512
POOL_WINDOWS = (2, 4, 8, 16)
POOL_GROUP = D_POOL // len(POOL_WINDOWS)
POOL_HIST = max(POOL_WINDOWS) - 1
HIST_ROWS = POOL_HIST + 1
HEAD_DIM = 128
N_HEADS = D_HG // HEAD_DIM
D_IN = D_POOL + 4 * D_HG
EPS = 1e-6
LOG2_E = math.log2(math.e)
PROMPT_CHUNK = 64
FFN_CHUNK = 1024
SEQ_GROUP = 8
HEAD_WAVE = 4

BASE_BLOCK = 16
SUBLANES = 8
assert BASE_BLOCK == 2 * SUBLANES
MAX_FACTOR_LOG = 40.0
MIDPOINT_MIN_LB = math.exp(-MAX_FACTOR_LOG / (BASE_BLOCK // 2 - 1))

V7X_VMEM_BYTES = 64 * 1024 * 1024
VMEM_LIMIT_BYTES = V7X_VMEM_BYTES - 8 * 1024 * 1024

_NT = (((1,), (1,)), ((), ()))
_TN = (((0,), (0,)), ((), ()))


def _rms(x, g):
    return x * lax.rsqrt(jnp.mean(x * x, axis=-1, keepdims=True) + EPS) * g


def _rms_split(x, g):
    return x * g, lax.rsqrt(jnp.mean(x * x, axis=-1, keepdims=True) + EPS)


def _silu(x):
    u = 0.5 * x
    return u + u * jnp.tanh(u)


def _bf(x):
    return x.astype(jnp.bfloat16)


def _dot(a, b, dims=None):
    if dims is None:
        return jnp.dot(a, b, preferred_element_type=jnp.float32)
    return lax.dot_general(a, b, dims, preferred_element_type=jnp.float32)


def _layer_kernel(x_ref, hist_ref, s0_ref, lbl_ref, g_pre_mix_ref, w_in_ref, w_pool_ref,
                  pool_scale_ref, g_hg_ref, w_out_ref, g_post_mix_ref, g_pre_ffn_ref,
                  w_gate_ref, w_up_ref, w_down_ref, g_post_ffn_ref,
                  y_ref, pool_out_ref, s_out_ref,
                  z_ref, ext_ref, ymix_ref,
                  *, nb, lt, n_valid, layer_idx):
    pid = pl.program_id(0)
    last = pl.num_programs(0) - 1

    @pl.when(pid == 0)
    def _init():
        ext_ref[:, 0:1, :] = jnp.zeros((nb, 1, D_POOL), jnp.float32)
        ext_ref[:, 1:HIST_ROWS, :] = hist_ref[...]
        for i in range(nb):
            for h in range(N_HEADS):
                s_out_ref[i, h] = s0_ref[i, h].T

    lbl = lbl_ref[...]
    lbe = jnp.exp(lbl - jnp.max(lbl, axis=0, keepdims=True))
    lb = (jnp.sum(lbe[:layer_idx + 1], axis=0, keepdims=True)
          / jnp.sum(lbe, axis=0, keepdims=True))

    t_loc = lax.broadcasted_iota(jnp.int32, (lt, 1), 0)
    pos = (pid * lt + t_loc + 1 + n_valid).astype(jnp.float32)
    inv_cnt = [1.0 / jnp.minimum(pos, float(w)) for w in POOL_WINDOWS]

    ti = lax.broadcasted_iota(jnp.int32, (lt, lt), 0)
    si = lax.broadcasted_iota(jnp.int32, (lt, lt), 1)
    pair_xor = jnp.where(ti > si, ti ^ si, 0)
    diag = ti == si
    in_base_block = (pair_xor < BASE_BLOCK) & (ti >= si)
    row = lax.broadcasted_iota(jnp.int32, (lt, HEAD_DIM), 0)
    g_hg = g_hg_ref[...]
    pool_scale = pool_scale_ref[...]

    def seq_rows(b):
        return pl.ds(b * lt if isinstance(b, int) else pl.multiple_of(b * lt, lt), lt)

    def half_block_scan(log2f):
        t = log2f.reshape(lt // SUBLANES, SUBLANES, HEAD_DIM)
        sub = lax.broadcasted_iota(jnp.int32, t.shape, 1)
        s = 1
        while s < SUBLANES:
            t = t + jnp.where(sub >= s, pltpu.roll(t, s, 1), 0.0)
            s *= 2
        tot = jnp.broadcast_to(t[:, SUBLANES - 1:SUBLANES, :], t.shape)
        return t.reshape(lt, HEAD_DIM), tot.reshape(lt, HEAD_DIM)

    def pair_products(q, k, log2f, midpoint):
        if midpoint:
            prods = []
            c, tot = half_block_scan(log2f)
            n = BASE_BLOCK // 2
        else:
            prods = [("diag", _dot(_bf(q), _bf(k), _NT))]
            c = log2f
            tot = log2f
            n = 1
        while n < lt:
            lower = (row & n) != 0
            if midpoint and 2 * n == BASE_BLOCK:
                a = jnp.where(lower, c, c - tot)
                prods.append(("base", _dot(_bf(q * jnp.exp2(a)), _bf(k * jnp.exp2(-a)), _NT)))
            elif not midpoint or n >= BASE_BLOCK:
                prods.append((n, _dot(_bf(q * jnp.exp2(c)), _bf(k * jnp.exp2(tot - c)), _NT)))
            dn = pltpu.roll(tot, n, 0)
            up = pltpu.roll(tot, lt - n, 0)
            c = c + jnp.where(lower, dn, 0.0)
            tot = tot + jnp.where(lower, dn, up)
            n *= 2
        return prods, c, tot

    def pair_scores(prods):
        attn = None
        for which, p in prods:
            if which == "diag":
                attn = jnp.where(diag, p, 0.0)
            elif which == "base":
                attn = jnp.where(in_base_block, p, 0.0)
            else:
                attn = jnp.where((pair_xor >= which) & (pair_xor < 2 * which), p, attn)
        return attn

    def pool_mixer(b):
        rows = seq_rows(b)
        for g, w in enumerate(POOL_WINDOWS):
            cols = slice(g * POOL_GROUP, (g + 1) * POOL_GROUP)
            e = ext_ref[b, :, cols]
            win = e
            span = 1
            while span < w:
                win = win + pltpu.roll(win, span, 0)
                span *= 2
            cur = e[HIST_ROWS:]
            d = win[HIST_ROWS:] * inv_cnt[g] - cur
            yp = _dot(_bf(d), w_pool_ref[g]) * pool_scale[:, cols]
            ymix_ref[rows, cols] = _bf(yp)
        ext_ref[b, 0:HIST_ROWS, :] = ext_ref[b, lt:lt + HIST_ROWS, :]

    def head_products(b, hd, midpoint):
        rows = seq_rows(b)
        c0 = hd * HEAD_DIM
        zq = z_ref[rows, c0:c0 + HEAD_DIM]
        zf = z_ref[rows, D_HG + c0:D_HG + c0 + HEAD_DIM]
        v = _bf(z_ref[rows, 2 * D_HG + c0:2 * D_HG + c0 + HEAD_DIM])
        lbh = lb[:, c0:c0 + HEAD_DIM]

        q = _silu(zq)
        fgate = lbh + (1.0 - lbh) * (0.5 + 0.5 * jnp.tanh(0.5 * zf))
        k = 1.0 - fgate
        log2f = jnp.log(fgate) * LOG2_E

        prods, c, tot = pair_products(q, k, log2f, midpoint)

        st = s_out_ref[b, hd]
        o_state = _dot(_bf(q * jnp.exp2(c)), _bf(st), _NT)
        k_dec = _bf(k * jnp.exp2(tot - c))
        s_out_ref[b, hd] = st * jnp.exp2(tot[0:1, :]) + _dot(v, k_dec, _TN)
        return prods, v, o_state

    def head_output(b, hd, attn, v, o_state):
        rows = seq_rows(b)
        c0 = hd * HEAD_DIM
        zg = z_ref[rows, 3 * D_HG + c0:3 * D_HG + c0 + HEAD_DIM]
        o = _rms(_dot(_bf(attn), v) + o_state, g_hg) * _silu(zg)
        ymix_ref[rows, D_POOL + c0:D_POOL + c0 + HEAD_DIM] = _bf(o)

    def hgrn_heads(units, midpoint):
        parts = [head_products(b, hd, midpoint) for b, hd in units]
        for (b, hd), (prods, v, o_state) in zip(units, parts):
            head_output(b, hd, pair_scores(prods), v, o_state)

    def in_proj(s0, s1):
        xg, scale = _rms_split(x_ref[s0:s1].reshape((s1 - s0) * lt, D_MODEL), g_pre_mix_ref[...])
        h = _bf(xg)
        ext_ref[s0:s1, HIST_ROWS:HIST_ROWS + lt, :] = (
            _dot(h, w_in_ref[:, 0:D_POOL]) * scale).reshape(s1 - s0, lt, D_POOL)
        z_ref[s0 * lt:s1 * lt, :] = _dot(h, w_in_ref[:, D_POOL:D_IN]) * scale
        for b in range(s0, s1):
            pool_mixer(b)

    def channel_mix(bounds):
        groups = range(len(bounds))
        rows = [slice(s0 * lt, s1 * lt) for s0, s1 in bounds]
        yproj = [_dot(ymix_ref[rows[gi], :], w_out_ref[...]) for gi in groups]
        x1 = [x_ref[bounds[gi][0]:bounds[gi][1]].reshape(-1, D_MODEL)
              + _rms(yproj[gi], g_post_mix_ref[...]) for gi in groups]
        split = [_rms_split(x1[gi], g_pre_ffn_ref[...]) for gi in groups]
        h2 = [_bf(split[gi][0]) for gi in groups]
        f = [None for _ in groups]
        for c0 in range(0, w_gate_ref.shape[1], FFN_CHUNK):
            cols = slice(c0, min(c0 + FFN_CHUNK, w_gate_ref.shape[1]))
            for gi in groups:
                gate = _dot(h2[gi], w_gate_ref[:, cols]) * split[gi][1]
                upp = _dot(h2[gi], w_up_ref[:, cols]) * split[gi][1]
                act = _bf(_silu(gate) * upp)
                part = _dot(act, w_down_ref[cols, :])
                f[gi] = part if f[gi] is None else f[gi] + part
        for gi, (s0, s1) in enumerate(bounds):
            y_ref[s0:s1] = (x1[gi] + _rms(f[gi], g_post_ffn_ref[...])).reshape(s1 - s0, lt, D_MODEL)

    def step(midpoint):
        in_proj(0, nb)
        def seq_group(i, carry):
            for w0 in range(0, SEQ_GROUP, HEAD_WAVE):
                hgrn_heads([(i * SEQ_GROUP + w0 + j, hd) for j in range(HEAD_WAVE)
                            for hd in range(N_HEADS)], midpoint)

        lax.fori_loop(0, nb // SEQ_GROUP, seq_group, None)
        channel_mix([(0, nb // 2), (nb // 2, nb)])

    lax.cond(jnp.min(lb) >= MIDPOINT_MIN_LB,
             functools.partial(step, True), functools.partial(step, False))

    @pl.when(pid == last)
    def _carry_out():
        pool_out_ref[...] = ext_ref[:, 1:HIST_ROWS, :]
        for i in range(nb):
            for hd in range(N_HEADS):
                s_out_ref[i, hd] = s_out_ref[i, hd].T


def _const_spec(shape):
    zeros = (0,) * len(shape)
    return pl.BlockSpec(shape, lambda i: zeros, pipeline_mode=pl.Buffered(1))


def _layer_call(x, hist, s0, lb_logits, weights, *, lt, n_valid, layer_idx):
    nb, seq, _ = x.shape
    assert seq % lt == 0 and lt % BASE_BLOCK == 0 and lt >= HIST_ROWS and (lt & (lt - 1)) == 0
    assert nb % SEQ_GROUP == 0 and SEQ_GROUP % HEAD_WAVE == 0 and nb % 2 == 0
    m = nb * lt
    (g_pre_mix, w_in, w_pool, pool_scale, g_hg, w_out, g_post_mix, g_pre_ffn,
     w_gate, w_up, w_down, g_post_ffn) = weights
    consts = (hist, s0, lb_logits, g_pre_mix, w_in, w_pool, pool_scale, g_hg, w_out,
              g_post_mix, g_pre_ffn, w_gate, w_up, w_down, g_post_ffn)
    x_spec = pl.BlockSpec((nb, lt, D_MODEL), lambda i: (0, i, 0))
    kern = functools.partial(_layer_kernel, nb=nb, lt=lt, n_valid=n_valid, layer_idx=layer_idx)
    return pl.pallas_call(
        kern,
        grid=(seq // lt,),
        in_specs=[x_spec] + [_const_spec(c.shape) for c in consts],
        out_specs=[x_spec, _const_spec((nb, POOL_HIST, D_POOL)),
                   _const_spec((nb, N_HEADS, HEAD_DIM, HEAD_DIM))],
        out_shape=[jax.ShapeDtypeStruct(x.shape, x.dtype),
                   jax.ShapeDtypeStruct((nb, POOL_HIST, D_POOL), x.dtype),
                   jax.ShapeDtypeStruct((nb, N_HEADS, HEAD_DIM, HEAD_DIM), jnp.float32)],
        scratch_shapes=[pltpu.VMEM((m, 4 * D_HG), jnp.float32),
                        pltpu.VMEM((nb, HIST_ROWS + lt, D_POOL), jnp.float32),
                        pltpu.VMEM((m, D_MODEL), jnp.bfloat16)],
        compiler_params=pltpu.CompilerParams(dimension_semantics=("arbitrary",),
                                             vmem_limit_bytes=VMEM_LIMIT_BYTES),
        name=f"layer_lt{lt}",
    )(x, *consts)


def kernel(x_prompt, x_sample, cache_pool, state_hgrn, g_pre_mix, w_in, w_pool, pool_scale,
           lb_logits, g_hg_norm, w_out, g_post_mix, g_pre_ffn, w_gate, w_up, w_down, g_post_ffn):
    depth = w_in.shape[0]
    xp, xs = x_prompt, x_sample
    pools_p, hgrn_p, pools_s, hgrn_s = [], [], [], []
    for l in range(depth):
        row = lambda a: a[l][None, :]
        weights = (row(g_pre_mix), _bf(w_in[l]), _bf(w_pool[l]), row(pool_scale), row(g_hg_norm),
                   _bf(w_out[l]), row(g_post_mix), row(g_pre_ffn), _bf(w_gate[l]), _bf(w_up[l]),
                   _bf(w_down[l]), row(g_post_ffn))
        hist0 = jnp.zeros((xp.shape[0], POOL_HIST, D_POOL), xp.dtype)
        s00 = jnp.zeros((xp.shape[0], N_HEADS, HEAD_DIM, HEAD_DIM), jnp.float32)
        xp, pool_p, s_p = _layer_call(xp, hist0, s00, lb_logits, weights,
                                      lt=PROMPT_CHUNK, n_valid=0, layer_idx=l)
        xs, pool_s, s_s = _layer_call(xs, cache_pool[l], state_hgrn[l], lb_logits, weights,
                                      lt=xs.shape[1], n_valid=POOL_HIST, layer_idx=l)
        pools_p.append(pool_p)
        hgrn_p.append(s_p)
        pools_s.append(pool_s)
        hgrn_s.append(s_s)
    stack = lambda per_layer: per_layer[0][None] if depth == 1 else jnp.stack(per_layer)
    return (xp, xs, stack(pools_p), stack(hgrn_p), stack(pools_s), stack(hgrn_s))
```

```python
import functools
import math

import jax
import jax.numpy as jnp
from jax import lax
from jax.experimental import pallas as pl
from jax.experimental.pallas import tpu as pltpu

D_MODEL = 1024
D_POOL = 512
D_HG = 512
POOL_WINDOWS = (2, 4, 8, 16)
POOL_GROUP = D_POOL // len(POOL_WINDOWS)
POOL_HIST = max(POOL_WINDOWS) - 1
HIST_ROWS = POOL_HIST + 1
HEAD_DIM = 128
N_HEADS = D_HG // HEAD_DIM
D_IN = D_POOL + 4 * D_HG
EPS = 1e-6
LOG2_E = math.log2(math.e)
PROMPT_CHUNK = 64
FFN_CHUNK = 1024
SEQ_GROUP = 8
HEAD_WAVE = 4

BASE_BLOCK = 16
SUBLANES = 8
assert BASE_BLOCK == 2 * SUBLANES
MAX_FACTOR_LOG = 40.0
MIDPOINT_MIN_LB = math.exp(-MAX_FACTOR_LOG / (BASE_BLOCK // 2 - 1))

V7X_VMEM_BYTES = 64 * 1024 * 1024
VMEM_LIMIT_BYTES = V7X_VMEM_BYTES - 8 * 1024 * 1024

_NT = (((1,), (1,)), ((), ()))
_TN = (((0,), (0,)), ((), ()))


def _rms(x, g):
    return x * lax.rsqrt(jnp.mean(x * x, axis=-1, keepdims=True) + EPS) * g


def _rms_split(x, g):
    return x * g, lax.rsqrt(jnp.mean(x * x, axis=-1, keepdims=True) + EPS)


def _silu(x):
    u = 0.5 * x
    return u + u * jnp.tanh(u)


def _bf(x):
    return x.astype(jnp.bfloat16)


def _dot(a, b, dims=None):
    if dims is None:
        return jnp.dot(a, b, preferred_element_type=jnp.float32)
    return lax.dot_general(a, b, dims, preferred_element_type=jnp.float32)


def _layer_kernel(x_ref, hist_ref, s0_ref, lbl_ref, g_pre_mix_ref, w_in_ref, w_pool_ref,
                  pool_scale_ref, g_hg_ref, w_out_ref, g_post_mix_ref, g_pre_ffn_ref,
                  w_gate_ref, w_up_ref, w_down_ref, g_post_ffn_ref,
                  y_ref, pool_out_ref, s_out_ref,
                  z_ref, ext_ref, ymix_ref, midpoint_ref,
                  *, nb, lt, n_valid, layer_idx):
    pid = pl.program_id(0)
    last = pl.num_programs(0) - 1

    lbl = lbl_ref[...]
    lbe = jnp.exp(lbl - jnp.max(lbl, axis=0, keepdims=True))
    lb = (jnp.sum(lbe[:layer_idx + 1], axis=0, keepdims=True)
          / jnp.sum(lbe, axis=0, keepdims=True))

    @pl.when(pid == 0)
    def _init():
        ext_ref[:, 0:1, :] = jnp.zeros((nb, 1, D_POOL), jnp.float32)
        ext_ref[:, 1:HIST_ROWS, :] = hist_ref[...]
        for i in range(nb):
            for h in range(N_HEADS):
                s_out_ref[i, h] = s0_ref[i, h].T
        midpoint_ref[0] = (jnp.min(lb) >= MIDPOINT_MIN_LB).astype(jnp.int32)

    t_loc = lax.broadcasted_iota(jnp.int32, (lt, 1), 0)
    pos = (pid * lt + t_loc + 1 + n_valid).astype(jnp.float32)
    inv_cnt = [1.0 / jnp.minimum(pos, float(w)) for w in POOL_WINDOWS]

    ti = lax.broadcasted_iota(jnp.int32, (lt, lt), 0)
    si = lax.broadcasted_iota(jnp.int32, (lt, lt), 1)
    pair_xor = jnp.where(ti > si, ti ^ si, 0)
    diag = ti == si
    in_base_block = (pair_xor < BASE_BLOCK) & (ti >= si)
    row = lax.broadcasted_iota(jnp.int32, (lt, HEAD_DIM), 0)
    g_hg = g_hg_ref[...]
    pool_scale = pool_scale_ref[...]

    def seq_rows(b):
        return pl.ds(b * lt if isinstance(b, int) else pl.multiple_of(b * lt, lt), lt)

    def half_block_scan(log2f):
        t = log2f.reshape(lt // SUBLANES, SUBLANES, HEAD_DIM)
        sub = lax.broadcasted_iota(jnp.int32, t.shape, 1)
        s = 1
        while s < SUBLANES:
            t = t + jnp.where(sub >= s, pltpu.roll(t, s, 1), 0.0)
            s *= 2
        tot = jnp.broadcast_to(t[:, SUBLANES - 1:SUBLANES, :], t.shape)
        return t.reshape(lt, HEAD_DIM), tot.reshape(lt, HEAD_DIM)

    def pair_products(q, k, log2f, midpoint):
        if midpoint:
            prods = []
            c, tot = half_block_scan(log2f)
            n = BASE_BLOCK // 2
        else:
            prods = [("diag", _dot(_bf(q), _bf(k), _NT))]
            c = log2f
            tot = log2f
            n = 1
        while n < lt:
            lower = (row & n) != 0
            if midpoint and 2 * n == BASE_BLOCK:
                a = jnp.where(lower, c, c - tot)
                prods.append(("base", _dot(_bf(q * jnp.exp2(a)), _bf(k * jnp.exp2(-a)), _NT)))
            elif not midpoint or n >= BASE_BLOCK:
                prods.append((n, _dot(_bf(q * jnp.exp2(c)), _bf(k * jnp.exp2(tot - c)), _NT)))
            dn = pltpu.roll(tot, n, 0)
            up = pltpu.roll(tot, lt - n, 0)
            c = c + jnp.where(lower, dn, 0.0)
            tot = tot + jnp.where(lower, dn, up)
            n *= 2
        return prods, c, tot

    def pair_scores(prods):
        attn = None
        for which, p in prods:
            if which == "diag":
                attn = jnp.where(diag, p, 0.0)
            elif which == "base":
                attn = jnp.where(in_base_block, p, 0.0)
            else:
                attn = jnp.where((pair_xor >= which) & (pair_xor < 2 * which), p, attn)
        return attn

    def pool_mixer(b):
        rows = seq_rows(b)
        for g, w in enumerate(POOL_WINDOWS):
            cols = slice(g * POOL_GROUP, (g + 1) * POOL_GROUP)
            e = ext_ref[b, :, cols]
            win = e
            span = 1
            while span < w:
                win = win + pltpu.roll(win, span, 0)
                span *= 2
            cur = e[HIST_ROWS:]
            d = win[HIST_ROWS:] * inv_cnt[g] - cur
            yp = _dot(_bf(d), w_pool_ref[g]) * pool_scale[:, cols]
            ymix_ref[rows, cols] = _bf(yp)
        ext_ref[b, 0:HIST_ROWS, :] = ext_ref[b, lt:lt + HIST_ROWS, :]

    def head_products(b, hd, midpoint):
        rows = seq_rows(b)
        c0 = hd * HEAD_DIM
        zq = z_ref[rows, c0:c0 + HEAD_DIM]
        zf = z_ref[rows, D_HG + c0:D_HG + c0 + HEAD_DIM]
        v = _bf(z_ref[rows, 2 * D_HG + c0:2 * D_HG + c0 + HEAD_DIM])
        lbh = lb[:, c0:c0 + HEAD_DIM]

        q = _silu(zq)
        fgate = lbh + (1.0 - lbh) * (0.5 + 0.5 * jnp.tanh(0.5 * zf))
        k = 1.0 - fgate
        log2f = jnp.log(fgate) * LOG2_E

        prods, c, tot = pair_products(q, k, log2f, midpoint)

        st = s_out_ref[b, hd]
        o_state = _dot(_bf(q * jnp.exp2(c)), _bf(st), _NT)
        k_dec = _bf(k * jnp.exp2(tot - c))
        s_out_ref[b, hd] = st * jnp.exp2(tot[0:1, :]) + _dot(v, k_dec, _TN)
        return prods, v, o_state

    def head_output(b, hd, attn, v, o_state):
        rows = seq_rows(b)
        c0 = hd * HEAD_DIM
        zg = z_ref[rows, 3 * D_HG + c0:3 * D_HG + c0 + HEAD_DIM]
        o = _rms(_dot(_bf(attn), v) + o_state, g_hg) * _silu(zg)
        ymix_ref[rows, D_POOL + c0:D_POOL + c0 + HEAD_DIM] = _bf(o)

    def hgrn_heads(units, midpoint):
        parts = [head_products(b, hd, midpoint) for b, hd in units]
        for (b, hd), (prods, v, o_state) in zip(units, parts):
            head_output(b, hd, pair_scores(prods), v, o_state)

    def in_proj(s0, s1):
        xg, scale = _rms_split(x_ref[s0:s1].reshape((s1 - s0) * lt, D_MODEL), g_pre_mix_ref[...])
        h = _bf(xg)
        ext_ref[s0:s1, HIST_ROWS:HIST_ROWS + lt, :] = (
            _dot(h, w_in_ref[:, 0:D_POOL]) * scale).reshape(s1 - s0, lt, D_POOL)
        z_ref[s0 * lt:s1 * lt, :] = _dot(h, w_in_ref[:, D_POOL:D_IN]) * scale
        for b in range(s0, s1):
            pool_mixer(b)

    def channel_mix(bounds):
        groups = range(len(bounds))
        rows = [slice(s0 * lt, s1 * lt) for s0, s1 in bounds]
        yproj = [_dot(ymix_ref[rows[gi], :], w_out_ref[...]) for gi in groups]
        x1 = [x_ref[bounds[gi][0]:bounds[gi][1]].reshape(-1, D_MODEL)
              + _rms(yproj[gi], g_post_mix_ref[...]) for gi in groups]
        split = [_rms_split(x1[gi], g_pre_ffn_ref[...]) for gi in groups]
        h2 = [_bf(split[gi][0]) for gi in groups]
        f = [None for _ in groups]
        for c0 in range(0, w_gate_ref.shape[1], FFN_CHUNK):
            cols = slice(c0, min(c0 + FFN_CHUNK, w_gate_ref.shape[1]))
            for gi in groups:
                gate = _dot(h2[gi], w_gate_ref[:, cols]) * split[gi][1]
                upp = _dot(h2[gi], w_up_ref[:, cols]) * split[gi][1]
                act = _bf(_silu(gate) * upp)
                part = _dot(act, w_down_ref[cols, :])
                f[gi] = part if f[gi] is None else f[gi] + part
        for gi, (s0, s1) in enumerate(bounds):
            y_ref[s0:s1] = (x1[gi] + _rms(f[gi], g_post_ffn_ref[...])).reshape(s1 - s0, lt, D_MODEL)

    def step(midpoint):
        in_proj(0, nb)
        def seq_group(i, carry):
            for w0 in range(0, SEQ_GROUP, HEAD_WAVE):
                hgrn_heads([(i * SEQ_GROUP + w0 + j, hd) for j in range(HEAD_WAVE)
                            for hd in range(N_HEADS)], midpoint)

        lax.fori_loop(0, nb // SEQ_GROUP, seq_group, None)
        channel_mix([(0, nb // 2), (nb // 2, nb)])

    lax.cond(midpoint_ref[0] == 1, functools.partial(step, True), functools.partial(step, False))

    @pl.when(pid == last)
    def _carry_out():
        pool_out_ref[...] = ext_ref[:, 1:HIST_ROWS, :]
        for i in range(nb):
            for hd in range(N_HEADS):
                s_out_ref[i, hd] = s_out_ref[i, hd].T


def _const_spec(shape):
    zeros = (0,) * len(shape)
    return pl.BlockSpec(shape, lambda i: zeros, pipeline_mode=pl.Buffered(1))


def _layer_call(x, hist, s0, lb_logits, weights, *, lt, n_valid, layer_idx):
    nb, seq, _ = x.shape
    assert seq % lt == 0 and lt % BASE_BLOCK == 0 and lt >= HIST_ROWS and (lt & (lt - 1)) == 0
    assert nb % SEQ_GROUP == 0 and SEQ_GROUP % HEAD_WAVE == 0 and nb % 2 == 0
    m = nb * lt
    (g_pre_mix, w_in, w_pool, pool_scale, g_hg, w_out, g_post_mix, g_pre_ffn,
     w_gate, w_up, w_down, g_post_ffn) = weights
    consts = (hist, s0, lb_logits, g_pre_mix, w_in, w_pool, pool_scale, g_hg, w_out,
              g_post_mix, g_pre_ffn, w_gate, w_up, w_down, g_post_ffn)
    x_spec = pl.BlockSpec((nb, lt, D_MODEL), lambda i: (0, i, 0))
    kern = functools.partial(_layer_kernel, nb=nb, lt=lt, n_valid=n_valid, layer_idx=layer_idx)
    return pl.pallas_call(
        kern,
        grid=(seq // lt,),
        in_specs=[x_spec] + [_const_spec(c.shape) for c in consts],
        out_specs=[x_spec, _const_spec((nb, POOL_HIST, D_POOL)),
                   _const_spec((nb, N_HEADS, HEAD_DIM, HEAD_DIM))],
        out_shape=[jax.ShapeDtypeStruct(x.shape, x.dtype),
                   jax.ShapeDtypeStruct((nb, POOL_HIST, D_POOL), x.dtype),
                   jax.ShapeDtypeStruct((nb, N_HEADS, HEAD_DIM, HEAD_DIM), jnp.float32)],
        scratch_shapes=[pltpu.VMEM((m, 4 * D_HG), jnp.float32),
                        pltpu.VMEM((nb, HIST_ROWS + lt, D_POOL), jnp.float32),
                        pltpu.VMEM((m, D_MODEL), jnp.bfloat16),
                        pltpu.SMEM((1,), jnp.int32)],
        compiler_params=pltpu.CompilerParams(dimension_semantics=("arbitrary",),
                                             vmem_limit_bytes=VMEM_LIMIT_BYTES),
        name=f"layer_lt{lt}",
    )(x, *consts)


def kernel(x_prompt, x_sample, cache_pool, state_hgrn, g_pre_mix, w_in, w_pool, pool_scale,
           lb_logits, g_hg_norm, w_out, g_post_mix, g_pre_ffn, w_gate, w_up, w_down, g_post_ffn):
    depth = w_in.shape[0]
    xp, xs = x_prompt, x_sample
    pools_p, hgrn_p, pools_s, hgrn_s = [], [], [], []
    for l in range(depth):
        row = lambda a: a[l][None, :]
        weights = (row(g_pre_mix), _bf(w_in[l]), _bf(w_pool[l]), row(pool_scale), row(g_hg_norm),
                   _bf(w_out[l]), row(g_post_mix), row(g_pre_ffn), _bf(w_gate[l]), _bf(w_up[l]),
                   _bf(w_down[l]), row(g_post_ffn))
        hist0 = jnp.zeros((xp.shape[0], POOL_HIST, D_POOL), xp.dtype)
        s00 = jnp.zeros((xp.shape[0], N_HEADS, HEAD_DIM, HEAD_DIM), jnp.float32)
        xp, pool_p, s_p = _layer_call(xp, hist0, s00, lb_logits, weights,
                                      lt=PROMPT_CHUNK, n_valid=0, layer_idx=l)
        xs, pool_s, s_s = _layer_call(xs, cache_pool[l], state_hgrn[l], lb_logits, weights,
                                      lt=xs.shape[1], n_valid=POOL_HIST, layer_idx=l)
        pools_p.append(pool_p)
        hgrn_p.append(s_p)
        pools_s.append(pool_s)
        hgrn_s.append(s_s)
    stack = lambda per_layer: per_layer[0][None] if depth == 1 else jnp.stack(per_layer)
    return (xp, xs, stack(pools_p), stack(hgrn_p), stack(pools_s), stack(hgrn_s))
```
